```python
import math
import jax
import jax.numpy as jnp
from jax import lax
import numpy as np


D_MODEL = 2048
BATCH = 4
SEQ = 4096
DEPTH = 2

HEAD_DIM = 128
W_A = D_MODEL // 4
CONV_A = 3
DH_B = 64
DV_B = 2 * DH_B
H_B = (3 * D_MODEL // 8) // DV_B
W_B = H_B * DV_B
ROT_DIM = DH_B // 4
ROPE_THETA = 500000.0
Q_BLOCK = 128
DK_C = HEAD_DIM
DV_C = HEAD_DIM
H_C = (3 * D_MODEL // 8) // DV_C
W_C = H_C * DV_C
CONV_C = 3
CHUNK = 64
N_BRANCH = 3
N_EXPERTS = 16
CAPACITY_FACTOR = 2
EXPERT_FF = D_MODEL // 2
EPS = 1e-6

SPLIT_SIZES = (W_A, W_A, W_A,
               H_B * 2 * DH_B, H_B * 2 * DH_B, W_B,
               H_C * DK_C, H_C * DK_C, W_C, W_C,
               H_C, H_C, H_C, H_C)
N_IN = sum(SPLIT_SIZES)

kernel_name = 'hybrid_gated_parallel_encoder'


def rms_norm(x, gain):
    xf = x.astype(jnp.float32)
    y = xf * lax.rsqrt(jnp.mean(xf * xf, axis=-1, keepdims=True) + EPS)
    return (y * gain.astype(jnp.float32)).astype(x.dtype)


def l2_normalize(x):
    xf = x.astype(jnp.float32)
    return (xf * lax.rsqrt(jnp.sum(xf * xf, axis=-1, keepdims=True) + EPS)).astype(x.dtype)


def centred_depthwise_conv(x, w):
    width = w.shape[0]
    r = width // 2
    T = x.shape[1]
    xp = jnp.pad(x, ((0, 0), (r, r), (0, 0)))
    return sum(xp[:, i:i + T] * w[i] for i in range(width))


def partial_rotary(x, cos, sin):
    half = ROT_DIM // 2
    c = cos[:, :, None, None, :].astype(x.dtype)
    s = sin[:, :, None, None, :].astype(x.dtype)
    x1 = x[..., :half]
    x2 = x[..., half:ROT_DIM]
    return jnp.concatenate([x1 * c - x2 * s, x2 * c + x1 * s, x[..., ROT_DIM:]], axis=-1)


def short_conv_mixer(b, c, v, conv_w):
    return b * centred_depthwise_conv(c * v, conv_w)


def blockwise_diff_attention(q, k, v, lam):
    B_, T, H = q.shape[:3]
    nb = T // Q_BLOCK
    qb = q.reshape(B_, nb, Q_BLOCK, H, 2, DH_B).transpose(1, 0, 3, 4, 2, 5)
    kt = k.transpose(0, 2, 3, 1, 4)
    vt = v.transpose(0, 2, 1, 3)
    scale = DH_B ** -0.5

    def one_block(qi):
        s = jnp.einsum('bhcqd,bhckd->bhcqk', qi, kt).astype(jnp.float32) * scale
        p = jax.nn.softmax(s, axis=-1)
        a = p[:, :, 0] - lam * p[:, :, 1]
        return jnp.einsum('bhqk,bhkd->bhqd', a.astype(vt.dtype), vt)

    o = lax.map(one_block, qb)
    return o.transpose(1, 0, 3, 2, 4).reshape(B_, T, H, DV_B)


def diff_attention_mixer(q, k, v, cos, sin, q_gain, k_gain, lq1, lk1, lq2, lk2, sub_gain, lam_init):
    B_, T, _ = q.shape
    q = partial_rotary(rms_norm(q.reshape(B_, T, H_B, 2, DH_B), q_gain), cos, sin)
    k = partial_rotary(rms_norm(k.reshape(B_, T, H_B, 2, DH_B), k_gain), cos, sin)
    v = v.reshape(B_, T, H_B, DV_B)
    f32 = jnp.float32
    lam = (jnp.exp(jnp.sum(lq1.astype(f32) * lk1.astype(f32)))
           - jnp.exp(jnp.sum(lq2.astype(f32) * lk2.astype(f32))) + lam_init)
    o = blockwise_diff_attention(q, k, v, lam)
    o = rms_norm(o, sub_gain) * (1.0 - lam_init)
    return o.reshape(B_, T, W_B)


def chunk_gated_delta_rule(q, k, v, g, beta):
    B_, T, H, Dk = q.shape
    Dv = v.shape[-1]
    n = T // CHUNK
    f32 = jnp.float32

    def to_chunks(t):
        return t.astype(f32).reshape(B_, n, CHUNK, H, -1).transpose(0, 3, 1, 2, 4)

    q, k, v = to_chunks(q), to_chunks(k), to_chunks(v)
    g = g.astype(f32).reshape(B_, n, CHUNK, H).transpose(0, 3, 1, 2)
    beta = beta.astype(f32).reshape(B_, n, CHUNK, H).transpose(0, 3, 1, 2)
    g = jnp.cumsum(g, axis=-1)
    incl = jnp.tril(jnp.ones((CHUNK, CHUNK), dtype=bool))
    strict = jnp.tril(jnp.ones((CHUNK, CHUNK), dtype=bool), k=-1)
    diff = g[..., :, None] - g[..., None, :]
    decay = jnp.where(incl, jnp.exp(jnp.where(incl, diff, 0.0)), 0.0)
    k_beta = k * beta[..., None]
    lower = jnp.where(strict, jnp.einsum('bhncd,bhnsd->bhncs', k_beta, k) * decay, 0.0)
    rhs = jnp.concatenate([v * beta[..., None], k_beta * jnp.exp(g)[..., None]], axis=-1)
    sol = lax.linalg.triangular_solve(jnp.eye(CHUNK, dtype=f32) + lower, rhs,
                                      left_side=True, lower=True, unit_diagonal=True)
    u, w = sol[..., :Dv], sol[..., Dv:]
    attn = jnp.einsum('bhncd,bhnsd->bhncs', q, k) * decay
    q_dec = q * jnp.exp(g)[..., None]
    k_dec = k * jnp.exp(g[..., -1:] - g)[..., None]
    g_end = jnp.exp(g[..., -1])

    def step(state, xs):
        u_i, w_i, q_i, a_i, k_i, ge_i = xs
        v_new = u_i - jnp.einsum('bhck,bhkv->bhcv', w_i, state)
        o_i = jnp.einsum('bhck,bhkv->bhcv', q_i, state) + jnp.einsum('bhcs,bhsv->bhcv', a_i, v_new)
        state = state * ge_i[..., None, None] + jnp.einsum('bhck,bhcv->bhkv', k_i, v_new)
        return state, o_i

    xs = tuple(jnp.moveaxis(t, 2, 0) for t in (u, w, q_dec, attn, k_dec, g_end))
    s0 = jnp.zeros((B_, H, Dk, Dv), f32)
    _, o = lax.scan(step, s0, xs)
    return o.transpose(1, 0, 3, 2, 4).reshape(B_, T, H, Dv)


def log_decay(a, a_log, dt_bias):
    f32 = jnp.float32
    return -jnp.exp(a_log.astype(f32)) * jax.nn.softplus(a.astype(f32) + dt_bias.astype(f32))


def gated_deltanet_mixer(q, k, v, g_out, beta_f, beta_b, a_f, a_b, conv_w,
                         a_log_f, a_log_b, dt_bias_f, dt_bias_b, o_gain):
    B_, T, _ = q.shape
    qkv = jax.nn.silu(centred_depthwise_conv(jnp.concatenate([q, k, v], axis=-1), conv_w))
    q, k, v = jnp.split(qkv, [H_C * DK_C, 2 * H_C * DK_C], axis=-1)
    q = l2_normalize(q.reshape(B_, T, H_C, DK_C)) * (DK_C ** -0.5)
    k = l2_normalize(k.reshape(B_, T, H_C, DK_C))
    v = v.reshape(B_, T, H_C, DV_C)
    o_fwd = chunk_gated_delta_rule(q, k, v, log_decay(a_f, a_log_f, dt_bias_f),
                                   jax.nn.sigmoid(beta_f.astype(jnp.float32)))
    rev = lambda t: jnp.flip(t, axis=1)
    o_bwd = rev(chunk_gated_delta_rule(rev(q), rev(k), rev(v), rev(log_decay(a_b, a_log_b, dt_bias_b)),
                                       rev(jax.nn.sigmoid(beta_b.astype(jnp.float32)))))
    o = rms_norm(o_fwd + o_bwd, o_gain) * jax.nn.silu(g_out.reshape(B_, T, H_C, DV_C).astype(jnp.float32))
    return o.astype(g_out.dtype).reshape(B_, T, W_C)


def expert_choice_ffn(h, w_router, w_gate, w_up, w_down):
    B_, T, _ = h.shape
    cap = CAPACITY_FACTOR * T // N_EXPERTS
    aff = jax.nn.softmax(jnp.einsum('btd,de->bte', h, w_router).astype(jnp.float32), axis=-1)
    weight, idx = lax.top_k(jnp.swapaxes(aff, 1, 2), cap)
    b_idx = jnp.arange(B_)[:, None, None]
    xs = h[b_idx, idx]
    hid = jax.nn.silu(jnp.einsum('becd,edf->becf', xs, w_gate)) * jnp.einsum('becd,edf->becf', xs, w_up)
    y = jnp.einsum('becf,efd->becd', hid, w_down) * weight[..., None].astype(h.dtype)
    return jnp.zeros_like(h).at[b_idx, idx].add(y)


def setup_inputs(seed: int = 0) -> dict:
    key = jax.random.key(seed)
    ks = iter(jax.random.split(key, 32))
    f32 = jnp.float32
    L, D = DEPTH, D_MODEL

    def normal(shape, scale):
        return scale * jax.random.normal(next(ks), shape, f32)

    def gain(shape):
        return 1.0 + normal(shape, 0.02)

    def dt_bias(shape):
        dt = jnp.exp(jax.random.uniform(next(ks), shape, f32, math.log(1e-3), math.log(1e-1)))
        return dt + jnp.log(-jnp.expm1(-dt))

    def a_log(shape):
        return jnp.log(jax.random.uniform(next(ks), shape, f32, 1.0, 16.0))

    x = normal((BATCH, SEQ, D), 1.0)
    positions = jnp.broadcast_to(jnp.arange(SEQ, dtype=jnp.int32), (BATCH, SEQ))
    return {
        'x': x,
        'positions': positions,
        'norm_mix': gain((L, D)),
        'w_in': normal((L, D, N_IN), D ** -0.5),
        'conv_a': normal((L, CONV_A, W_A), CONV_A ** -0.5),
        'q_norm': gain((L, DH_B)),
        'k_norm': gain((L, DH_B)),
        'lambda_q1': normal((L, DH_B), 0.1),
        'lambda_k1': normal((L, DH_B), 0.1),
        'lambda_q2': normal((L, DH_B), 0.1),
        'lambda_k2': normal((L, DH_B), 0.1),
        'subln': gain((L, DV_B)),
        'conv_c': normal((L, CONV_C, 2 * H_C * DK_C + W_C), CONV_C ** -0.5),
        'a_log_f': a_log((L, H_C)),
        'a_log_b': a_log((L, H_C)),
        'dt_bias_f': dt_bias((L, H_C)),
        'dt_bias_b': dt_bias((L, H_C)),
        'o_norm': gain((L, DV_C)),
        'w_out_a': normal((L, W_A, D), W_A ** -0.5),
        'w_out_b': normal((L, W_B, D), W_B ** -0.5),
        'w_out_c': normal((L, W_C, D), W_C ** -0.5),
        'w_gate': normal((L, D, N_BRANCH * D), D ** -0.5),
        'b_gate': normal((L, N_BRANCH * D), 0.02),
        'w_o': normal((L, D, D), D ** -0.5),
        'norm_ffn': gain((L, D)),
        'w_router': normal((L, D, N_EXPERTS), D ** -0.5),
        'w_e_gate': normal((L, N_EXPERTS, D, EXPERT_FF), D ** -0.5),
        'w_e_up': normal((L, N_EXPERTS, D, EXPERT_FF), D ** -0.5),
        'w_e_down': normal((L, N_EXPERTS, EXPERT_FF, D), EXPERT_FF ** -0.5),
    }


def reference(x, positions, norm_mix, w_in, conv_a, q_norm, k_norm, lambda_q1, lambda_k1,
              lambda_q2, lambda_k2, subln, conv_c, a_log_f, a_log_b, dt_bias_f, dt_bias_b,
              o_norm, w_out_a, w_out_b, w_out_c, w_gate, b_gate, w_o, norm_ffn, w_router,
              w_e_gate, w_e_up, w_e_down):
    inv_freq = ROPE_THETA ** (-jnp.arange(0, ROT_DIM, 2, dtype=jnp.float32) / ROT_DIM)
    ang = positions.astype(jnp.float32)[..., None] * inv_freq
    cos, sin = jnp.cos(ang), jnp.sin(ang)
    split_at = np.cumsum(SPLIT_SIZES)[:-1].tolist()
    for l in range(DEPTH):
        lam_init = 0.8 - 0.6 * math.exp(-0.3 * l)
        xn = rms_norm(x, norm_mix[l])
        (b_a, c_a, v_a, q_b, k_b, v_b, q_c, k_c, v_c, g_c,
         beta_f, beta_b, a_f, a_b) = jnp.split(xn @ w_in[l], split_at, axis=-1)
        y_a = short_conv_mixer(b_a, c_a, v_a, conv_a[l]) @ w_out_a[l]
        y_b = diff_attention_mixer(q_b, k_b, v_b, cos, sin, q_norm[l], k_norm[l],
                                   lambda_q1[l], lambda_k1[l], lambda_q2[l], lambda_k2[l],
                                   subln[l], lam_init) @ w_out_b[l]
        y_c = gated_deltanet_mixer(q_c, k_c, v_c, g_c, beta_f, beta_b, a_f, a_b, conv_c[l],
                                   a_log_f[l], a_log_b[l], dt_bias_f[l], dt_bias_b[l],
                                   o_norm[l]) @ w_out_c[l]
        g_a, g_b, g_cc = jnp.split(jax.nn.sigmoid(xn @ w_gate[l] + b_gate[l]), N_BRANCH, axis=-1)
        x = x + (g_a * y_a + g_b * y_b + g_cc * y_c) @ w_o[l]
        x = x + expert_choice_ffn(rms_norm(x, norm_ffn[l]), w_router[l],
                                  w_e_gate[l], w_e_up[l], w_e_down[l])
    return x
```

```python
import functools
import math

import jax
import jax.numpy as jnp
from jax import lax
from jax.experimental import pallas as pl
from jax.experimental.pallas import tpu as pltpu

F32 = jnp.float32
BF16 = jnp.bfloat16
I32 = jnp.int32

D_MODEL = 2048
HEAD_DIM = 128
W_A = D_MODEL // 4
DH_B = 64
DV_B = 2 * DH_B
H_B = (3 * D_MODEL // 8) // DV_B
W_B = H_B * DV_B
ROT_DIM = DH_B // 4
ROPE_THETA = 500000.0
DK_C = HEAD_DIM
DV_C = HEAD_DIM
H_C = (3 * D_MODEL // 8) // DV_C
W_C = H_C * DV_C
CHUNK = 64
N_BRANCH = 3
N_EXPERTS = 16
CAPACITY_FACTOR = 2
EXPERT_FF = D_MODEL // 2
EPS = 1e-6

N_MAIN = 3 * W_A + 3 * W_B + 4 * W_C
N_SMALL = 4 * H_C
COL_A = 0
COL_B = 3 * W_A
COL_C = COL_B + 3 * W_B

V7X_LANES = 128
V7X_VMEM_LIMIT_BYTES = 56 * 1024 * 1024
ROW_WIN = 64
NEG_POS = -(1 << 20)


def _cparams(sem, vmem=None):
    return pltpu.CompilerParams(dimension_semantics=sem, vmem_limit_bytes=vmem)


def _split_dot(a, b):
    hi = a.astype(BF16)
    lo = (a - hi.astype(F32)).astype(BF16)
    return (jnp.dot(hi, b, preferred_element_type=F32)
            + jnp.dot(lo, b, preferred_element_type=F32))


def _norm_matmul_kernel(x_ref, g_ref, w_ref, o_ref, xn_ref):
    @pl.when(pl.program_id(1) == 0)
    def _():
        x = x_ref[...]
        xn = x * lax.rsqrt(jnp.mean(x * x, axis=-1, keepdims=True) + EPS) * g_ref[...]
        xn_ref[...] = xn.astype(BF16)

    o_ref[...] = jnp.dot(xn_ref[...], w_ref[...], preferred_element_type=F32).astype(o_ref.dtype)


def _norm_matmul(x, gain, w, out_dtype, tm, tn):
    n, d = x.shape
    nout = w.shape[1]
    return pl.pallas_call(
        _norm_matmul_kernel,
        grid=(n // tm, nout // tn),
        in_specs=[pl.BlockSpec((tm, d), lambda i, j: (i, 0)),
                  pl.BlockSpec((1, d), lambda i, j: (0, 0)),
                  pl.BlockSpec((d, tn), lambda i, j: (0, j))],
        out_specs=pl.BlockSpec((tm, tn), lambda i, j: (i, j)),
        out_shape=jax.ShapeDtypeStruct((n, nout), out_dtype),
        scratch_shapes=[pltpu.VMEM((tm, d), BF16)],
        compiler_params=_cparams(("parallel", "arbitrary"), V7X_VMEM_LIMIT_BYTES),
        name="norm_matmul",
    )(x, gain.reshape(1, d), w)


def _conv3(x, w):
    t = x.shape[0]
    row = lax.broadcasted_iota(I32, x.shape, 0)
    prev = jnp.where(row == 0, 0.0, pltpu.roll(x, 1, 0))
    nxt = jnp.where(row == t - 1, 0.0, pltpu.roll(x, t - 1, 0))
    return prev * w[0:1, :] + x * w[1:2, :] + nxt * w[2:3, :]


def _mixer_a_kernel(b_ref, c_ref, v_ref, w_ref, o_ref):
    cv = c_ref[0].astype(F32) * v_ref[0].astype(F32)
    o_ref[0] = (b_ref[0].astype(F32) * _conv3(cv, w_ref[...])).astype(o_ref.dtype)


def _mixer_a(main, conv_w):
    bsz, t, _ = main.shape
    nblk = W_A // V7X_LANES
    c0 = COL_A // V7X_LANES
    spec = lambda off: pl.BlockSpec((1, t, V7X_LANES), lambda b, c: (b, 0, c0 + off + c))
    return pl.pallas_call(
        _mixer_a_kernel,
        grid=(bsz, nblk),
        in_specs=[spec(0), spec(nblk), spec(2 * nblk),
                  pl.BlockSpec((3, V7X_LANES), lambda b, c: (0, c))],
        out_specs=pl.BlockSpec((1, t, V7X_LANES), lambda b, c: (b, 0, c)),
        out_shape=jax.ShapeDtypeStruct((bsz, t, W_A), BF16),
        compiler_params=_cparams(("parallel", "parallel")),
        name="mixer_a",
    )(main, main, main, conv_w)


def _gdn_prep_kernel(x_ref, w_ref, o_ref):
    c = pl.program_id(1)
    y = _conv3(x_ref[0].astype(F32), w_ref[...])
    y = y * jax.nn.sigmoid(y)
    inv = lax.rsqrt(jnp.sum(y * y, axis=-1, keepdims=True) + EPS)
    scale = jnp.where(c < H_C, inv * (DK_C ** -0.5), jnp.where(c < 2 * H_C, inv, 1.0))
    o_ref[0] = (y * scale).astype(o_ref.dtype)


def _gdn_prep(main, conv_w):
    bsz, t, _ = main.shape
    nblk = 3 * W_C // V7X_LANES
    c0 = COL_C // V7X_LANES
    return pl.pallas_call(
        _gdn_prep_kernel,
        grid=(bsz, nblk),
        in_specs=[pl.BlockSpec((1, t, V7X_LANES), lambda b, c: (b, 0, c0 + c)),
                  pl.BlockSpec((3, V7X_LANES), lambda b, c: (0, c))],
        out_specs=pl.BlockSpec((1, t, V7X_LANES), lambda b, c: (b, 0, c)),
        out_shape=jax.ShapeDtypeStruct((bsz, t, 3 * W_C), BF16),
        compiler_params=_cparams(("parallel", "parallel")),
        name="gdn_prep",
    )(main, conv_w)


def _rope_kernel(pos_ref, c_ref, s1_ref, s2_ref):
    pos = pos_ref[0].astype(F32)
    lane = lax.broadcasted_iota(I32, (1, V7X_LANES), 1)
    l64 = lane % DH_B
    half = ROT_DIM // 2
    fidx = jnp.where(l64 < half, l64, l64 - half).astype(F32)
    inv_freq = jnp.exp(fidx * (-math.log(ROPE_THETA) * 2.0 / ROT_DIM))
    ang = pos * inv_freq
    cos, sin = jnp.cos(ang), jnp.sin(ang)
    first = l64 < half
    second = (l64 >= half) & (l64 < ROT_DIM)
    c_ref[0] = jnp.where(first | second, cos, 1.0)
    s1_ref[0] = jnp.where(first, -sin, 0.0)
    s2_ref[0] = jnp.where(second, sin, 0.0)


def _rope_tables(positions):
    bsz, t = positions.shape
    shp = jax.ShapeDtypeStruct((bsz, t, V7X_LANES), F32)
    spec = pl.BlockSpec((1, t, V7X_LANES), lambda b: (b, 0, 0))
    return pl.pallas_call(
        _rope_kernel,
        grid=(bsz,),
        in_specs=[pl.BlockSpec((1, t, 1), lambda b: (b, 0, 0))],
        out_specs=[spec, spec, spec],
        out_shape=[shp, shp, shp],
        compiler_params=_cparams(("parallel",)),
        name="rope_tables",
    )(positions.reshape(bsz, t, 1))


def _qk_prep(x, gain, cos, s1, s2):
    r = lax.broadcasted_iota(I32, (V7X_LANES, V7X_LANES), 0) // DH_B
    c = lax.broadcasted_iota(I32, (V7X_LANES, V7X_LANES), 1) // DH_B
    blockdiag = (r == c).astype(BF16)
    ssq = _split_dot(x * x, blockdiag)
    xn = x * lax.rsqrt(ssq * (1.0 / DH_B) + EPS) * gain
    half = ROT_DIM // 2
    return xn * cos + pltpu.roll(xn, V7X_LANES - half, 1) * s1 + pltpu.roll(xn, half, 1) * s2


def _attn_kernel(lam_init, tq, tk, q_ref, k_ref, v_ref, c_ref, s1_ref, s2_ref,
                 qg_ref, kg_ref, lamv_ref, sub_ref, o_ref, kp_ref, vx_ref, acc_ref):
    qi = pl.program_id(2)
    t = kp_ref.shape[0]

    @pl.when(qi == 0)
    def _():
        kp_ref[...] = _qk_prep(k_ref[0].astype(F32), kg_ref[...],
                               c_ref[0], s1_ref[0], s2_ref[0]).astype(BF16)
        vx_ref[:, :V7X_LANES] = v_ref[0]
        vx_ref[:, V7X_LANES:] = jnp.ones((t, V7X_LANES), BF16)

    rows = pl.ds(pl.multiple_of(qi * tq, tq), tq)
    q = _qk_prep(q_ref[0].astype(F32), qg_ref[...],
                 c_ref[0, rows, :], s1_ref[0, rows, :], s2_ref[0, rows, :]) * (DH_B ** -0.5)
    lane = lax.broadcasted_iota(I32, (tq, V7X_LANES), 1)
    qsub = (jnp.where(lane < DH_B, q, 0.0).astype(BF16), jnp.where(lane >= DH_B, q, 0.0).astype(BF16))

    def scores(j):
        kj = kp_ref[j * tk:(j + 1) * tk, :]
        return [lax.dot_general(qc, kj, (((1,), (1,)), ((), ())), preferred_element_type=F32)
                for qc in qsub]

    acc_ref[...] = jnp.zeros_like(acc_ref)
    m = [jnp.full((tq, 1), -jnp.inf, F32)] * 2
    nxt = scores(0)
    for j in range(t // tk):
        cur = nxt
        if j + 1 < t // tk:
            nxt = scores(j + 1)
        vj = vx_ref[j * tk:(j + 1) * tk, :]
        for c in range(2):
            m_new = jnp.maximum(m[c], jnp.max(cur[c], axis=-1, keepdims=True))
            e = jnp.exp(cur[c] - m_new).astype(BF16)
            acc_ref[c] = acc_ref[c] * jnp.exp(m[c] - m_new) + jnp.dot(e, vj, preferred_element_type=F32)
            m[c] = m_new
    pv = [acc_ref[c, :, :V7X_LANES] for c in range(2)]
    l = [acc_ref[c, :, V7X_LANES:V7X_LANES + 1] for c in range(2)]
    lv = lamv_ref[...]
    lam = (jnp.exp(jnp.sum(lv[0:1] * lv[1:2], axis=-1, keepdims=True))
           - jnp.exp(jnp.sum(lv[2:3] * lv[3:4], axis=-1, keepdims=True)) + lam_init)
    o = pv[0] / l[0] - lam * (pv[1] / l[1])
    o = o * lax.rsqrt(jnp.mean(o * o, axis=-1, keepdims=True) + EPS) * sub_ref[...]
    o_ref[0] = (o * (1.0 - lam_init)).astype(o_ref.dtype)


def _diff_attention(main, tables, q_gain, k_gain, lamv, subln, lam_init, tq, tk):
    bsz, t, _ = main.shape
    cq = COL_B // V7X_LANES
    ck = cq + H_B
    cv = ck + H_B
    tab = pl.BlockSpec((1, t, V7X_LANES), lambda b, h, i: (b, 0, 0))
    vec = pl.BlockSpec((1, V7X_LANES), lambda b, h, i: (0, 0))
    return pl.pallas_call(
        functools.partial(_attn_kernel, lam_init, tq, tk),
        grid=(bsz, H_B, t // tq),
        in_specs=[pl.BlockSpec((1, tq, V7X_LANES), lambda b, h, i: (b, i, cq + h)),
                  pl.BlockSpec((1, t, V7X_LANES), lambda b, h, i: (b, 0, ck + h)),
                  pl.BlockSpec((1, t, V7X_LANES), lambda b, h, i: (b, 0, cv + h)),
                  tab, tab, tab, vec, vec,
                  pl.BlockSpec((4, V7X_LANES), lambda b, h, i: (0, 0)),
                  vec],
        out_specs=pl.BlockSpec((1, tq, V7X_LANES), lambda b, h, i: (b, i, h)),
        out_shape=jax.ShapeDtypeStruct((bsz, t, W_B), BF16),
        scratch_shapes=[pltpu.VMEM((t, V7X_LANES), BF16),
                        pltpu.VMEM((t, 2 * V7X_LANES), BF16),
                        pltpu.VMEM((2, tq, 2 * V7X_LANES), F32)],
        compiler_params=_cparams(("parallel", "parallel", "arbitrary"), V7X_VMEM_LIMIT_BYTES),
        name="diff_attention",
    )(main, main, main, *tables, q_gain, k_gain, lamv, subln)


def _gdn_kernel(rev, lb, q_ref, k_ref, v_ref, sm_ref, alog_ref, dtb_ref, o_ref, s_ref):
    @pl.when(pl.program_id(1) == 0)
    def _():
        s_ref[...] = jnp.zeros_like(s_ref)

    off_b = H_C if rev else 0
    off_a = 3 * H_C if rev else 2 * H_C
    ri = lax.broadcasted_iota(I32, (CHUNK, CHUNK), 0)
    ci = lax.broadcasted_iota(I32, (CHUNK, CHUNK), 1)
    incl = (ri <= ci) if rev else (ri >= ci)
    strict = (ri < ci) if rev else (ri > ci)
    cum_mat = incl.astype(BF16)
    eye = (ri == ci).astype(F32)
    last = 0 if rev else CHUNK - 1

    nchunk = lb // CHUNK
    order = list(range(nchunk - 1, -1, -1) if rev else range(nchunk))
    units = [(c, h) for c in order for h in range(H_C)]
    nt = (((1,), (1,)), ((), ()))
    dot = functools.partial(jnp.dot, preferred_element_type=F32)

    gates = {}
    for c in order:
        rows = slice(c * CHUNK, (c + 1) * CHUNK)
        sm = sm_ref[0, rows, :]
        g_all = -jnp.exp(alog_ref[...]) * jax.nn.softplus(sm + dtb_ref[...])
        gc_all = _split_dot_left(cum_mat, g_all)
        gates[c] = (jax.nn.sigmoid(sm), gc_all, gc_all.T)

    pre = {}
    for c, h in units:
        rows = slice(c * CHUNK, (c + 1) * CHUNK)
        lanes = slice(h * HEAD_DIM, (h + 1) * HEAD_DIM)
        beta_all, gc_all, gct_all = gates[c]
        q = q_ref[0, rows, lanes].astype(F32)
        k = k_ref[0, rows, lanes]
        kf = k.astype(F32)
        beta = beta_all[:, off_b + h:off_b + h + 1]
        gcol = gc_all[:, off_a + h:off_a + h + 1]
        grow = gct_all[off_a + h:off_a + h + 1, :]
        glast = gcol[last:last + 1, :]
        decay = jnp.where(incl, jnp.exp(jnp.where(incl, gcol - grow, 0.0)), 0.0)
        kb = kf * beta
        eg = jnp.exp(gcol)
        rhs = jnp.concatenate([v_ref[0, rows, lanes].astype(F32) * beta, kb * eg], axis=-1).astype(BF16)
        pre[c, h] = dict(k=k, kb=kb, decay=decay, rhs=rhs, glast=glast,
                         q=q, q_dec=q * eg, k_dec=(kf * jnp.exp(glast - gcol)).astype(BF16))

    for u in units:
        d = pre[u]
        res = lax.dot_general(jnp.concatenate([d["kb"], d["q"]], axis=0).astype(BF16), d["k"], nt,
                              preferred_element_type=F32)
        d["low"] = jnp.where(strict, res[:CHUNK] * d["decay"], 0.0)
        d["attn"] = (res[CHUNK:] * d["decay"]).astype(BF16)
    for u in units:
        d = pre[u]
        lb16 = d["low"].astype(BF16)
        d["n"] = dot(lb16, lb16)
        d["p"] = eye - d["low"]
    for r in range(1, 6):
        for u in units:
            d = pre[u]
            nb16 = d["n"].astype(BF16)
            if r < 5:
                res = dot(jnp.concatenate([d["n"], d["p"]], axis=0).astype(BF16), nb16)
                d["n"] = res[:CHUNK]
                d["p"] = d["p"] + res[CHUNK:]
            else:
                d["p"] = d["p"] + dot(d["p"].astype(BF16), nb16)
    for u in units:
        d = pre[u]
        d["sol"] = dot(d["p"].astype(BF16), d["rhs"])

    st = {h: s_ref[h] for h in range(H_C)}
    for c in order:
        rows = slice(c * CHUNK, (c + 1) * CHUNK)
        res, vb = {}, {}
        for h in range(H_C):
            d = pre[c, h]
            lhs = jnp.concatenate([d["sol"][:, DV_C:], d["q_dec"]], axis=0).astype(BF16)
            res[h] = dot(lhs, st[h].astype(BF16))
        for h in range(H_C):
            vb[h] = (pre[c, h]["sol"][:, :DV_C] - res[h][:CHUNK]).astype(BF16)
        for h in range(H_C):
            d = pre[c, h]
            lanes = slice(h * HEAD_DIM, (h + 1) * HEAD_DIM)
            o_ref[0, rows, lanes] = res[h][CHUNK:] + dot(d["attn"], vb[h])
            st[h] = st[h] * jnp.exp(d["glast"]) + lax.dot_general(
                d["k_dec"], vb[h], (((0,), (0,)), ((), ())), preferred_element_type=F32)
    for h in range(H_C):
        s_ref[h] = st[h]


def _split_dot_left(a, b):
    hi = b.astype(BF16)
    lo = (b - hi.astype(F32)).astype(BF16)
    return (jnp.dot(a, hi, preferred_element_type=F32)
            + jnp.dot(a, lo, preferred_element_type=F32))


def _gdn_direction(qkv, small, a_log, dt_bias, rev, lb):
    bsz, t, _ = qkv.shape
    nb = t // lb
    blk = (lambda i: nb - 1 - i) if rev else (lambda i: i)
    off_a = 3 * H_C if rev else 2 * H_C
    pad = lambda vec: jnp.zeros((1, V7X_LANES), F32).at[0, off_a:off_a + H_C].set(vec)
    col = lambda j: pl.BlockSpec((1, lb, W_C), lambda b, i: (b, blk(i), j))
    vec = pl.BlockSpec((1, V7X_LANES), lambda b, i: (0, 0))
    return pl.pallas_call(
        functools.partial(_gdn_kernel, rev, lb),
        grid=(bsz, nb),
        in_specs=[col(0), col(1), col(2),
                  pl.BlockSpec((1, lb, V7X_LANES), lambda b, i: (b, blk(i), 0)),
                  vec, vec],
        out_specs=pl.BlockSpec((1, lb, W_C), lambda b, i: (b, blk(i), 0)),
        out_shape=jax.ShapeDtypeStruct((bsz, t, W_C), F32),
        scratch_shapes=[pltpu.VMEM((H_C, DK_C, DV_C), F32)],
        compiler_params=_cparams(("parallel", "arbitrary")),
        name="gdn_bwd" if rev else "gdn_fwd",
    )(qkv, qkv, qkv, small, pad(a_log), pad(dt_bias))


def _merge_kernel(tn, x_ref, gn_ref, ma_ref, mb_ref, of_ref, ob_ref, gc_ref, og_ref,
                  wout_ref, wg0_ref, wg1_ref, wg2_ref, bg0_ref, bg1_ref, bg2_ref, wo_ref,
                  o_ref, xn_ref, mix_ref):
    j = pl.program_id(1)

    @pl.when(j == 0)
    def _():
        x = x_ref[...]
        xn = x * lax.rsqrt(jnp.mean(x * x, axis=-1, keepdims=True) + EPS) * gn_ref[...]
        xn_ref[...] = xn.astype(BF16)
        mix_ref[:, 0:W_A] = ma_ref[...]
        mix_ref[:, W_A:W_A + W_B] = mb_ref[...]
        for h in range(H_C):
            lanes = slice(h * HEAD_DIM, (h + 1) * HEAD_DIM)
            o = of_ref[:, lanes] + ob_ref[:, lanes]
            o = o * lax.rsqrt(jnp.mean(o * o, axis=-1, keepdims=True) + EPS) * og_ref[...]
            g = gc_ref[:, lanes].astype(F32)
            c0 = W_A + W_B + h * HEAD_DIM
            mix_ref[:, c0:c0 + HEAD_DIM] = (o * (g * jax.nn.sigmoid(g))).astype(BF16)
        o_ref[...] = x

    xn = xn_ref[...]
    s = None
    for lo, hi, wg_ref, bg_ref in ((0, W_A, wg0_ref, bg0_ref),
                                   (W_A, W_A + W_B, wg1_ref, bg1_ref),
                                   (W_A + W_B, D_MODEL, wg2_ref, bg2_ref)):
        y = jnp.dot(mix_ref[:, lo:hi], wout_ref[lo:hi, :], preferred_element_type=F32)
        gate = jax.nn.sigmoid(jnp.dot(xn, wg_ref[...], preferred_element_type=F32) + bg_ref[...])
        s = gate * y if s is None else s + gate * y
    o_ref[...] += jnp.dot(s.astype(BF16), wo_ref[...], preferred_element_type=F32)


def _merge(x, norm_gain, mix_a, mix_b, o_f, o_b, main, o_gain, w_out, w_gate, b_gate, w_o, tm, tn):
    n, d = x.shape
    nj = d // tn
    cg = (COL_C + 3 * W_C) // W_C
    row = lambda w: pl.BlockSpec((tm, w), lambda i, j: (i, 0))
    wg = lambda br: pl.BlockSpec((d, tn), lambda i, j: (0, br * nj + j))
    bg = lambda br: pl.BlockSpec((1, tn), lambda i, j: (0, br * nj + j))
    return pl.pallas_call(
        functools.partial(_merge_kernel, tn),
        grid=(n // tm, nj),
        in_specs=[row(d),
                  pl.BlockSpec((1, d), lambda i, j: (0, 0)),
                  row(W_A), row(W_B), row(W_C), row(W_C),
                  pl.BlockSpec((tm, W_C), lambda i, j: (i, cg)),
                  pl.BlockSpec((1, HEAD_DIM), lambda i, j: (0, 0)),
                  pl.BlockSpec((d, tn), lambda i, j: (0, j)),
                  wg(0), wg(1), wg(2), bg(0), bg(1), bg(2),
                  pl.BlockSpec((tn, d), lambda i, j: (j, 0))],
        out_specs=pl.BlockSpec((tm, d), lambda i, j: (i, 0)),
        out_shape=jax.ShapeDtypeStruct((n, d), F32),
        scratch_shapes=[pltpu.VMEM((tm, d), BF16), pltpu.VMEM((tm, d), BF16)],
        compiler_params=_cparams(("parallel", "arbitrary"), V7X_VMEM_LIMIT_BYTES),
        name="merge",
    )(x, norm_gain.reshape(1, d), mix_a, mix_b, o_f, o_b, main, o_gain.reshape(1, HEAD_DIM),
      w_out, w_gate, w_gate, w_gate, b_gate, b_gate, b_gate, w_o)


def _router_kernel(x_ref, g_ref, wh_ref, wl_ref, h_ref, aff_ref):
    x = x_ref[...]
    h = x * lax.rsqrt(jnp.mean(x * x, axis=-1, keepdims=True) + EPS) * g_ref[...]
    h_ref[...] = h
    hh = h.astype(BF16)
    hl = (h - hh.astype(F32)).astype(BF16)
    logits = (jnp.dot(hh, wh_ref[...], preferred_element_type=F32)
              + jnp.dot(hl, wh_ref[...], preferred_element_type=F32)
              + jnp.dot(hh, wl_ref[...], preferred_element_type=F32))
    lane = lax.broadcasted_iota(I32, logits.shape, 1)
    valid = lane < N_EXPERTS
    logits = jnp.where(valid, logits, -jnp.inf)
    m = jnp.max(logits, axis=-1, keepdims=True)
    e = jnp.where(valid, jnp.exp(logits - m), 0.0)
    aff_ref[...] = e / jnp.sum(e, axis=-1, keepdims=True)


def _router(x, gain, w_router, tm):
    n, d = x.shape
    wpad = jnp.zeros((d, V7X_LANES), F32).at[:, :N_EXPERTS].set(w_router)
    w_hi = wpad.astype(BF16)
    w_lo = (wpad - w_hi.astype(F32)).astype(BF16)
    wspec = pl.BlockSpec((d, V7X_LANES), lambda i: (0, 0))
    return pl.pallas_call(
        _router_kernel,
        grid=(n // tm,),
        in_specs=[pl.BlockSpec((tm, d), lambda i: (i, 0)),
                  pl.BlockSpec((1, d), lambda i: (0, 0)), wspec, wspec],
        out_specs=[pl.BlockSpec((tm, d), lambda i: (i, 0)),
                   pl.BlockSpec((tm, V7X_LANES), lambda i: (i, 0))],
        out_shape=[jax.ShapeDtypeStruct((n, d), F32),
                   jax.ShapeDtypeStruct((n, V7X_LANES), F32)],
        compiler_params=_cparams(("parallel",), V7X_VMEM_LIMIT_BYTES),
        name="router",
    )(x, gain.reshape(1, d), w_hi, w_lo)


CUM_BLK = 256


def _select_kernel(cap, aff_ref, idx_ref, pos_ref, wsel_ref, starts_ref, cum_ref):
    t = aff_ref.shape[1]
    aff = aff_ref[0]

    def search(i, lo):
        cand = lo | jnp.left_shift(jnp.int32(1), 30 - i)
        cnt = jnp.sum((aff >= pltpu.bitcast(cand, F32)).astype(I32), axis=0, keepdims=True)
        return jnp.where(cnt >= cap, cand, lo)

    thr_bits = lax.fori_loop(0, 31, search, jnp.zeros((1, V7X_LANES), I32))
    thr = pltpu.bitcast(thr_bits, F32)
    gt = aff > thr
    eq = aff == thr
    need = (cap - jnp.sum(gt.astype(I32), axis=0, keepdims=True)).astype(F32)

    ri = lax.broadcasted_iota(I32, (CUM_BLK, CUM_BLK), 0)
    ci = lax.broadcasted_iota(I32, (CUM_BLK, CUM_BLK), 1)
    tri = (ri >= ci).astype(BF16)

    def cumsum_rows(mask_f32_of_block, store):
        off = jnp.zeros((1, V7X_LANES), F32)
        for blk in range(t // CUM_BLK):
            rows = slice(blk * CUM_BLK, (blk + 1) * CUM_BLK)
            c = jnp.dot(tri, mask_f32_of_block(rows).astype(BF16), preferred_element_type=F32) + off
            store(rows, c)
            off = c[CUM_BLK - 1:CUM_BLK, :]

    def store_cum(rows, c):
        cum_ref[rows, :] = c

    cumsum_rows(lambda rows: eq[rows].astype(F32), store_cum)
    sel = gt | (eq & (cum_ref[...] <= need))
    self32 = sel.astype(F32)
    cumsum_rows(lambda rows: self32[rows], store_cum)
    cum = cum_ref[...]
    pos_ref[0] = jnp.where(sel, cum.astype(I32) - 1, NEG_POS)
    wsel_ref[0] = jnp.where(sel, aff, 0.0)

    nstart = starts_ref.shape[1]
    tblk = t // (nstart - 1)
    starts_ref[0, 0:1, :] = jnp.zeros((1, V7X_LANES), I32)
    for tb in range(1, nstart):
        starts_ref[0, tb:tb + 1, :] = cum[tb * tblk - 1:tb * tblk, :].astype(I32)

    jj = lax.broadcasted_iota(I32, (1, cap), 1).astype(F32)
    ones = jnp.ones((8, CUM_BLK), BF16)
    for e in range(N_EXPERTS):
        acc = jnp.zeros((8, cap), F32)
        for blk in range(t // CUM_BLK):
            col = cum_ref[blk * CUM_BLK:(blk + 1) * CUM_BLK, e:e + 1]
            acc = acc + jnp.dot(ones, (col <= jj).astype(BF16), preferred_element_type=F32)
        idx_ref[0, e:e + 1, :] = acc[0:1, :].astype(I32)


def _select(aff, bsz, t, n_tok_blk):
    cap = CAPACITY_FACTOR * t // N_EXPERTS
    aff3 = aff.reshape(bsz, t, V7X_LANES)
    tl = pl.BlockSpec((1, t, V7X_LANES), lambda b: (b, 0, 0))
    return pl.pallas_call(
        functools.partial(_select_kernel, cap),
        grid=(bsz,),
        in_specs=[tl],
        out_specs=[pl.BlockSpec((1, N_EXPERTS, cap), lambda b: (b, 0, 0)), tl, tl,
                   pl.BlockSpec((1, n_tok_blk + 1, V7X_LANES), lambda b: (b, 0, 0))],
        out_shape=[jax.ShapeDtypeStruct((bsz, N_EXPERTS, cap), I32),
                   jax.ShapeDtypeStruct((bsz, t, V7X_LANES), I32),
                   jax.ShapeDtypeStruct((bsz, t, V7X_LANES), F32),
                   jax.ShapeDtypeStruct((bsz, n_tok_blk + 1, V7X_LANES), I32)],
        scratch_shapes=[pltpu.VMEM((t, V7X_LANES), F32)],
        compiler_params=_cparams(("parallel",), V7X_VMEM_LIMIT_BYTES),
        name="moe_select",
    )(aff3)


def _ffn_kernel(t, cap, idx_ref, h_hbm, wg_ref, wu_ref, wd_ref, y_ref, xbuf, sem):
    b = pl.program_id(1)

    def row_copy(j):
        tok = idx_ref[0, 0, j]
        return pltpu.make_async_copy(h_hbm.at[pl.ds(b * t + tok, 1), :],
                                     xbuf.at[pl.ds(j, 1), :], sem)

    def start(j, c):
        row_copy(j).start()
        return c

    def wait(j, c):
        row_copy(j).wait()
        return c

    lax.fori_loop(0, cap, start, 0)
    lax.fori_loop(0, cap, wait, 0)
    x = xbuf[...].astype(BF16)
    g = jnp.dot(x, wg_ref[0], preferred_element_type=F32)
    u = jnp.dot(x, wu_ref[0], preferred_element_type=F32)
    hid = (g * jax.nn.sigmoid(g) * u).astype(BF16)
    y_ref[0, 0] = jnp.dot(hid, wd_ref[0], preferred_element_type=F32).astype(y_ref.dtype)


def _expert_ffn(idx, h, w_gate, w_up, w_down, bsz, t):
    cap = idx.shape[-1]
    d = h.shape[1]
    idx3 = idx.reshape(bsz * N_EXPERTS, 1, cap)
    return pl.pallas_call(
        functools.partial(_ffn_kernel, t, cap),
        grid=(N_EXPERTS, bsz),
        in_specs=[pl.BlockSpec((1, 1, cap), lambda e, b: (b * N_EXPERTS + e, 0, 0),
                               memory_space=pltpu.SMEM),
                  pl.BlockSpec(memory_space=pl.ANY),
                  pl.BlockSpec((1, d, EXPERT_FF), lambda e, b: (e, 0, 0)),
                  pl.BlockSpec((1, d, EXPERT_FF), lambda e, b: (e, 0, 0)),
                  pl.BlockSpec((1, EXPERT_FF, d), lambda e, b: (e, 0, 0))],
        out_specs=pl.BlockSpec((1, 1, cap, d), lambda e, b: (b, e, 0, 0)),
        out_shape=jax.ShapeDtypeStruct((bsz, N_EXPERTS, cap, d), BF16),
        scratch_shapes=[pltpu.VMEM((cap, d), F32), pltpu.SemaphoreType.DMA],
        compiler_params=_cparams(("arbitrary", "arbitrary"), V7X_VMEM_LIMIT_BYTES),
        name="moe_ffn",
    )(idx3, h, w_gate, w_up, w_down)


def _combine_kernel(cap, tblk, starts_ref, x_ref, pos_ref, w_ref, y_hbm, o_ref, kbuf, sem):
    b = pl.program_id(0)
    tb = pl.program_id(1)
    o_ref[0] = x_ref[0]

    first, npass = [], jnp.int32(0)
    for e in range(N_EXPERTS):
        s0 = starts_ref[0, tb, e]
        s1 = starts_ref[0, tb + 1, e]
        al = (s0 // 16) * 16
        first.append(al)
        npass = jnp.maximum(npass, jnp.where(s1 > s0, (s1 - al + ROW_WIN - 1) // ROW_WIN, 0))

    def window_copy(e, off):
        return pltpu.make_async_copy(y_hbm.at[b, e, pl.ds(off, ROW_WIN), :],
                                     kbuf.at[pl.ds(e * ROW_WIN, ROW_WIN), :], sem)

    def one_pass(p, carry):
        lows = [first[e] + p * ROW_WIN for e in range(N_EXPERTS)]
        offs = [pl.multiple_of(jnp.minimum(lo, cap - ROW_WIN), 16) for lo in lows]
        for e in range(N_EXPERTS):
            window_copy(e, offs[e]).start()
        lane = lax.broadcasted_iota(I32, (tblk, V7X_LANES), 1)
        slot = lane % ROW_WIN
        head = lane < ROW_WIN
        tiles = []
        for g in range(N_EXPERTS * ROW_WIN // V7X_LANES):
            ea, eb = 2 * g, 2 * g + 1
            pe = jnp.where(head, pos_ref[0, :, ea:ea + 1], pos_ref[0, :, eb:eb + 1])
            we = jnp.where(head, w_ref[0, :, ea:ea + 1], w_ref[0, :, eb:eb + 1])
            off = jnp.where(head, offs[ea], offs[eb])
            low = jnp.where(head, lows[ea], lows[eb])
            hit = (pe == off + slot) & (pe >= low)
            tiles.append(jnp.where(hit, we, 0.0).astype(BF16))
        pmat = jnp.concatenate(tiles, axis=1)
        for e in range(N_EXPERTS):
            window_copy(e, offs[e]).wait()
        o_ref[0] += jnp.dot(pmat, kbuf[...], preferred_element_type=F32)
        return carry

    lax.fori_loop(0, npass, one_pass, 0)


def _combine(x, pos, wsel, starts, y, tblk):
    bsz, t, d = x.shape
    cap = y.shape[2]
    return pl.pallas_call(
        functools.partial(_combine_kernel, cap, tblk),
        grid=(bsz, t // tblk),
        in_specs=[pl.BlockSpec((1, t // tblk + 1, V7X_LANES), lambda b, i: (b, 0, 0),
                               memory_space=pltpu.SMEM),
                  pl.BlockSpec((1, tblk, d), lambda b, i: (b, i, 0)),
                  pl.BlockSpec((1, tblk, V7X_LANES), lambda b, i: (b, i, 0)),
                  pl.BlockSpec((1, tblk, V7X_LANES), lambda b, i: (b, i, 0)),
                  pl.BlockSpec(memory_space=pl.ANY)],
        out_specs=pl.BlockSpec((1, tblk, d), lambda b, i: (b, i, 0)),
        out_shape=jax.ShapeDtypeStruct((bsz, t, d), F32),
        scratch_shapes=[pltpu.VMEM((N_EXPERTS * ROW_WIN, d), BF16), pltpu.SemaphoreType.DMA],
        compiler_params=_cparams(("parallel", "arbitrary")),
        name="moe_combine",
    )(starts, x, pos, wsel, y)


def _tiles(n, t):
    pick = lambda want, total: want if total % want == 0 else total
    return dict(
        tm_proj=pick(1024, n), tn_proj=768,
        tq=pick(512, t), tk=pick(512, t), lb=pick(512, t),
        tm_merge=pick(512, n), tn_merge=256,
        tm_router=pick(512, n),
        tblk=pick(256, t),
    )


def kernel(x, positions, norm_mix, w_in, conv_a, q_norm, k_norm, lambda_q1, lambda_k1, lambda_q2, lambda_k2, subln, conv_c, a_log_f, a_log_b, dt_bias_f, dt_bias_b, o_norm, w_out_a, w_out_b, w_out_c, w_gate, b_gate, w_o, norm_ffn, w_router, w_e_gate, w_e_up, w_e_down):
    bsz, t, d = x.shape
    n = bsz * t
    depth = w_in.shape[0]
    ts = _tiles(n, t)
    tables = _rope_tables(positions)
    pad_small = jnp.zeros((d, V7X_LANES - N_SMALL), F32)
    tile2 = lambda g: jnp.concatenate([g, g]).reshape(1, V7X_LANES)
    pad64 = lambda vv: jnp.concatenate([vv, jnp.zeros_like(vv)])

    xf = x.reshape(n, d)
    for l in range(depth):
        lam_init = 0.8 - 0.6 * math.exp(-0.3 * l)
        w_main = w_in[l, :, :N_MAIN].astype(BF16)
        w_small = jnp.concatenate([w_in[l, :, N_MAIN:], pad_small], axis=1).astype(BF16)
        main = _norm_matmul(xf, norm_mix[l], w_main, BF16, ts["tm_proj"], ts["tn_proj"])
        small = _norm_matmul(xf, norm_mix[l], w_small, F32, ts["tm_proj"], V7X_LANES)
        main3 = main.reshape(bsz, t, N_MAIN)
        small3 = small.reshape(bsz, t, V7X_LANES)

        mix_a = _mixer_a(main3, conv_a[l])
        lamv = jnp.stack([pad64(lambda_q1[l]), pad64(lambda_k1[l]),
                          pad64(lambda_q2[l]), pad64(lambda_k2[l])])
        mix_b = _diff_attention(main3, tables, tile2(q_norm[l]), tile2(k_norm[l]), lamv,
                                subln[l].reshape(1, V7X_LANES), lam_init, ts["tq"], ts["tk"])
        qkv_c = _gdn_prep(main3, conv_c[l])
        o_f = _gdn_direction(qkv_c, small3, a_log_f[l], dt_bias_f[l], False, ts["lb"])
        o_b = _gdn_direction(qkv_c, small3, a_log_b[l], dt_bias_b[l], True, ts["lb"])

        w_out = jnp.concatenate([w_out_a[l], w_out_b[l], w_out_c[l]], axis=0).astype(BF16)
        x1 = _merge(xf, norm_mix[l], mix_a.reshape(n, W_A), mix_b.reshape(n, W_B),
                    o_f.reshape(n, W_C), o_b.reshape(n, W_C), main, o_norm[l],
                    w_out, w_gate[l].astype(BF16), b_gate[l].reshape(1, N_BRANCH * d),
                    w_o[l].astype(BF16), ts["tm_merge"], ts["tn_merge"])

        h, aff = _router(x1, norm_ffn[l], w_router[l], ts["tm_router"])
        idx, pos, wsel, starts = _select(aff, bsz, t, t // ts["tblk"])
        y = _expert_ffn(idx, h, w_e_gate[l].astype(BF16), w_e_up[l].astype(BF16),
                        w_e_down[l].astype(BF16), bsz, t)
        xf = _combine(x1.reshape(bsz, t, d), pos, wsel, starts, y, ts["tblk"]).reshape(n, d)
    return xf.reshape(bsz, t, d)
```

```python
import functools
import math

import jax
import jax.numpy as jnp
from jax import lax
from jax.experimental import pallas as pl
from jax.experimental.pallas import tpu as pltpu

F32 = jnp.float32
BF16 = jnp.bfloat16
I32 = jnp.int32

D_MODEL = 2048
HEAD_DIM = 128
W_A = D_MODEL // 4
DH_B = 64
DV_B = 2 * DH_B
H_B = (3 * D_MODEL // 8) // DV_B
W_B = H_B * DV_B
ROT_DIM = DH_B // 4
ROPE_THETA = 500000.0
DK_C = HEAD_DIM
DV_C = HEAD_DIM
H_C = (3 * D_MODEL // 8) // DV_C
W_C = H_C * DV_C
CHUNK = 64
N_BRANCH = 3
N_EXPERTS = 16
CAPACITY_FACTOR = 2
EXPERT_FF = D_MODEL // 2
EPS = 1e-6

N_MAIN = 3 * W_A + 3 * W_B + 4 * W_C
N_SMALL = 4 * H_C
COL_A = 0
COL_B = 3 * W_A
COL_C = COL_B + 3 * W_B

V7X_LANES = 128
V7X_VMEM_LIMIT_BYTES = 56 * 1024 * 1024
ROW_WIN = 64
NEG_POS = -(1 << 20)


def _cparams(sem, vmem=None):
    return pltpu.CompilerParams(dimension_semantics=sem, vmem_limit_bytes=vmem)


def _split_dot(a, b):
    hi = a.astype(BF16)
    lo = (a - hi.astype(F32)).astype(BF16)
    return (jnp.dot(hi, b, preferred_element_type=F32)
            + jnp.dot(lo, b, preferred_element_type=F32))


def _norm_matmul_kernel(x_ref, g_ref, w_ref, o_ref, xn_ref):
    @pl.when(pl.program_id(1) == 0)
    def _():
        x = x_ref[...]
        xn = x * lax.rsqrt(jnp.mean(x * x, axis=-1, keepdims=True) + EPS) * g_ref[...]
        xn_ref[...] = xn.astype(BF16)

    o_ref[...] = jnp.dot(xn_ref[...], w_ref[...].astype(BF16),
                         preferred_element_type=F32).astype(o_ref.dtype)


def _norm_matmul(x, gain, w, layer, nout, out_dtype, tm, tn):
    n, d = x.shape
    return pl.pallas_call(
        _norm_matmul_kernel,
        grid=(n // tm, nout // tn),
        in_specs=[pl.BlockSpec((tm, d), lambda i, j: (i, 0)),
                  pl.BlockSpec((1, d), lambda i, j: (0, 0)),
                  pl.BlockSpec((None, d, tn), lambda i, j: (layer, 0, j))],
        out_specs=pl.BlockSpec((tm, tn), lambda i, j: (i, j)),
        out_shape=jax.ShapeDtypeStruct((n, nout), out_dtype),
        scratch_shapes=[pltpu.VMEM((tm, d), BF16)],
        compiler_params=_cparams(("parallel", "arbitrary"), V7X_VMEM_LIMIT_BYTES),
        name="norm_matmul",
    )(x, gain.reshape(1, d), w)


def _conv3(x, w):
    t = x.shape[0]
    row = lax.broadcasted_iota(I32, x.shape, 0)
    prev = jnp.where(row == 0, 0.0, pltpu.roll(x, 1, 0))
    nxt = jnp.where(row == t - 1, 0.0, pltpu.roll(x, t - 1, 0))
    return prev * w[0:1, :] + x * w[1:2, :] + nxt * w[2:3, :]


def _mixer_a_kernel(b_ref, c_ref, v_ref, w_ref, o_ref):
    cv = c_ref[0].astype(F32) * v_ref[0].astype(F32)
    o_ref[0] = (b_ref[0].astype(F32) * _conv3(cv, w_ref[...])).astype(o_ref.dtype)


def _mixer_a(main, conv_w):
    bsz, t, _ = main.shape
    nblk = W_A // V7X_LANES
    c0 = COL_A // V7X_LANES
    spec = lambda off: pl.BlockSpec((1, t, V7X_LANES), lambda b, c: (b, 0, c0 + off + c))
    return pl.pallas_call(
        _mixer_a_kernel,
        grid=(bsz, nblk),
        in_specs=[spec(0), spec(nblk), spec(2 * nblk),
                  pl.BlockSpec((3, V7X_LANES), lambda b, c: (0, c))],
        out_specs=pl.BlockSpec((1, t, V7X_LANES), lambda b, c: (b, 0, c)),
        out_shape=jax.ShapeDtypeStruct((bsz, t, W_A), BF16),
        compiler_params=_cparams(("parallel", "parallel")),
        name="mixer_a",
    )(main, main, main, conv_w)


def _gdn_prep_kernel(x_ref, w_ref, o_ref):
    c = pl.program_id(1)
    y = _conv3(x_ref[0].astype(F32), w_ref[...])
    y = y * jax.nn.sigmoid(y)
    inv = lax.rsqrt(jnp.sum(y * y, axis=-1, keepdims=True) + EPS)
    scale = jnp.where(c < H_C, inv * (DK_C ** -0.5), jnp.where(c < 2 * H_C, inv, 1.0))
    o_ref[0] = (y * scale).astype(o_ref.dtype)


def _gdn_prep(main, conv_w):
    bsz, t, _ = main.shape
    nblk = 3 * W_C // V7X_LANES
    c0 = COL_C // V7X_LANES
    return pl.pallas_call(
        _gdn_prep_kernel,
        grid=(bsz, nblk),
        in_specs=[pl.BlockSpec((1, t, V7X_LANES), lambda b, c: (b, 0, c0 + c)),
                  pl.BlockSpec((3, V7X_LANES), lambda b, c: (0, c))],
        out_specs=pl.BlockSpec((1, t, V7X_LANES), lambda b, c: (b, 0, c)),
        out_shape=jax.ShapeDtypeStruct((bsz, t, 3 * W_C), BF16),
        compiler_params=_cparams(("parallel", "parallel")),
        name="gdn_prep",
    )(main, conv_w)


def _rope_kernel(pos_ref, c_ref, s1_ref, s2_ref):
    pos = pos_ref[0].astype(F32)
    lane = lax.broadcasted_iota(I32, (1, V7X_LANES), 1)
    l64 = lane % DH_B
    half = ROT_DIM // 2
    fidx = jnp.where(l64 < half, l64, l64 - half).astype(F32)
    inv_freq = jnp.exp(fidx * (-math.log(ROPE_THETA) * 2.0 / ROT_DIM))
    ang = pos * inv_freq
    cos, sin = jnp.cos(ang), jnp.sin(ang)
    first = l64 < half
    second = (l64 >= half) & (l64 < ROT_DIM)
    c_ref[0] = jnp.where(first | second, cos, 1.0)
    s1_ref[0] = jnp.where(first, -sin, 0.0)
    s2_ref[0] = jnp.where(second, sin, 0.0)


def _rope_tables(positions):
    bsz, t = positions.shape
    shp = jax.ShapeDtypeStruct((bsz, t, V7X_LANES), F32)
    spec = pl.BlockSpec((1, t, V7X_LANES), lambda b: (b, 0, 0))
    return pl.pallas_call(
        _rope_kernel,
        grid=(bsz,),
        in_specs=[pl.BlockSpec((1, t, 1), lambda b: (b, 0, 0))],
        out_specs=[spec, spec, spec],
        out_shape=[shp, shp, shp],
        compiler_params=_cparams(("parallel",)),
        name="rope_tables",
    )(positions.reshape(bsz, t, 1))


def _qk_prep(x, gain, cos, s1, s2):
    r = lax.broadcasted_iota(I32, (V7X_LANES, V7X_LANES), 0) // DH_B
    c = lax.broadcasted_iota(I32, (V7X_LANES, V7X_LANES), 1) // DH_B
    blockdiag = (r == c).astype(BF16)
    ssq = _split_dot(x * x, blockdiag)
    xn = x * lax.rsqrt(ssq * (1.0 / DH_B) + EPS) * gain
    half = ROT_DIM // 2
    return xn * cos + pltpu.roll(xn, V7X_LANES - half, 1) * s1 + pltpu.roll(xn, half, 1) * s2


def _attn_kernel(lam_init, tq, tk, q_ref, k_ref, v_ref, c_ref, s1_ref, s2_ref,
                 qg_ref, kg_ref, lamv_ref, sub_ref, o_ref, kp_ref, vx_ref, acc_ref):
    qi = pl.program_id(2)
    t = kp_ref.shape[0]

    @pl.when(qi == 0)
    def _():
        kp_ref[...] = _qk_prep(k_ref[0].astype(F32), kg_ref[...],
                               c_ref[0], s1_ref[0], s2_ref[0]).astype(BF16)
        vx_ref[:, :V7X_LANES] = v_ref[0]
        vx_ref[:, V7X_LANES:] = jnp.ones((t, V7X_LANES), BF16)

    rows = pl.ds(pl.multiple_of(qi * tq, tq), tq)
    q = _qk_prep(q_ref[0].astype(F32), qg_ref[...], c_ref[0, rows, :], s1_ref[0, rows, :],
                 s2_ref[0, rows, :]) * (DH_B ** -0.5 * math.log2(math.e))
    lane = lax.broadcasted_iota(I32, (tq, V7X_LANES), 1)
    qsub = (jnp.where(lane < DH_B, q, 0.0).astype(BF16), jnp.where(lane >= DH_B, q, 0.0).astype(BF16))

    def scores(j):
        kj = kp_ref[j * tk:(j + 1) * tk, :]
        return [lax.dot_general(qc, kj, (((1,), (1,)), ((), ())), preferred_element_type=F32)
                for qc in qsub]

    acc_ref[...] = jnp.zeros_like(acc_ref)
    m = [jnp.full((tq, 1), -jnp.inf, F32)] * 2
    nxt = scores(0)
    for j in range(t // tk):
        cur = nxt
        if j + 1 < t // tk:
            nxt = scores(j + 1)
        vj = vx_ref[j * tk:(j + 1) * tk, :]
        for c in range(2):
            m_new = jnp.maximum(m[c], jnp.max(cur[c], axis=-1, keepdims=True))
            e = jnp.exp2(cur[c] - m_new).astype(BF16)
            acc_ref[c] = acc_ref[c] * jnp.exp2(m[c] - m_new) + jnp.dot(e, vj, preferred_element_type=F32)
            m[c] = m_new
    pv = [acc_ref[c, :, :V7X_LANES] for c in range(2)]
    l = [acc_ref[c, :, V7X_LANES:V7X_LANES + 1] for c in range(2)]
    lv = lamv_ref[...]
    lam = (jnp.exp(jnp.sum(lv[0:1] * lv[1:2], axis=-1, keepdims=True))
           - jnp.exp(jnp.sum(lv[2:3] * lv[3:4], axis=-1, keepdims=True)) + lam_init)
    o = pv[0] / l[0] - lam * (pv[1] / l[1])
    o = o * lax.rsqrt(jnp.mean(o * o, axis=-1, keepdims=True) + EPS) * sub_ref[...]
    o_ref[0] = (o * (1.0 - lam_init)).astype(o_ref.dtype)


def _diff_attention(main, tables, q_gain, k_gain, lamv, subln, lam_init, tq, tk):
    bsz, t, _ = main.shape
    cq = COL_B // V7X_LANES
    ck = cq + H_B
    cv = ck + H_B
    tab = pl.BlockSpec((1, t, V7X_LANES), lambda b, h, i: (b, 0, 0))
    vec = pl.BlockSpec((1, V7X_LANES), lambda b, h, i: (0, 0))
    return pl.pallas_call(
        functools.partial(_attn_kernel, lam_init, tq, tk),
        grid=(bsz, H_B, t // tq),
        in_specs=[pl.BlockSpec((1, tq, V7X_LANES), lambda b, h, i: (b, i, cq + h)),
                  pl.BlockSpec((1, t, V7X_LANES), lambda b, h, i: (b, 0, ck + h)),
                  pl.BlockSpec((1, t, V7X_LANES), lambda b, h, i: (b, 0, cv + h)),
                  tab, tab, tab, vec, vec,
                  pl.BlockSpec((4, V7X_LANES), lambda b, h, i: (0, 0)),
                  vec],
        out_specs=pl.BlockSpec((1, tq, V7X_LANES), lambda b, h, i: (b, i, h)),
        out_shape=jax.ShapeDtypeStruct((bsz, t, W_B), BF16),
        scratch_shapes=[pltpu.VMEM((t, V7X_LANES), BF16),
                        pltpu.VMEM((t, 2 * V7X_LANES), BF16),
                        pltpu.VMEM((2, tq, 2 * V7X_LANES), F32)],
        compiler_params=_cparams(("parallel", "parallel", "arbitrary"), V7X_VMEM_LIMIT_BYTES),
        name="diff_attention",
    )(main, main, main, *tables, q_gain, k_gain, lamv, subln)


def _gdn_kernel(rev, lb, q_ref, k_ref, v_ref, sm_ref, alog_ref, dtb_ref, o_ref, s_ref):
    @pl.when(pl.program_id(1) == 0)
    def _():
        s_ref[...] = jnp.zeros_like(s_ref)

    off_b = H_C if rev else 0
    off_a = 3 * H_C if rev else 2 * H_C
    ri = lax.broadcasted_iota(I32, (CHUNK, CHUNK), 0)
    ci = lax.broadcasted_iota(I32, (CHUNK, CHUNK), 1)
    incl = (ri <= ci) if rev else (ri >= ci)
    strict = (ri < ci) if rev else (ri > ci)
    cum_mat = incl.astype(BF16)
    eye = (ri == ci).astype(F32)
    last = 0 if rev else CHUNK - 1

    nchunk = lb // CHUNK
    order = list(range(nchunk - 1, -1, -1) if rev else range(nchunk))
    units = [(c, h) for c in order for h in range(H_C)]
    nt = (((1,), (1,)), ((), ()))
    dot = functools.partial(jnp.dot, preferred_element_type=F32)

    gates = {}
    for c in order:
        rows = slice(c * CHUNK, (c + 1) * CHUNK)
        sm = sm_ref[0, rows, :]
        g_all = -jnp.exp(alog_ref[...]) * jax.nn.softplus(sm + dtb_ref[...])
        gc_all = _split_dot_left(cum_mat, g_all)
        gates[c] = (jax.nn.sigmoid(sm), gc_all, gc_all.T)

    pre = {}
    for c, h in units:
        rows = slice(c * CHUNK, (c + 1) * CHUNK)
        lanes = slice(h * HEAD_DIM, (h + 1) * HEAD_DIM)
        beta_all, gc_all, gct_all = gates[c]
        q = q_ref[0, rows, lanes].astype(F32)
        k = k_ref[0, rows, lanes]
        kf = k.astype(F32)
        beta = beta_all[:, off_b + h:off_b + h + 1]
        gcol = gc_all[:, off_a + h:off_a + h + 1]
        grow = gct_all[off_a + h:off_a + h + 1, :]
        glast = gcol[last:last + 1, :]
        decay = jnp.where(incl, jnp.exp(jnp.where(incl, gcol - grow, 0.0)), 0.0)
        kb = kf * beta
        eg = jnp.exp(gcol)
        rhs = jnp.concatenate([v_ref[0, rows, lanes].astype(F32) * beta, kb * eg], axis=-1).astype(BF16)
        pre[c, h] = dict(k=k, kb=kb, decay=decay, rhs=rhs, glast=glast,
                         q=q, q_dec=q * eg, k_dec=(kf * jnp.exp(glast - gcol)).astype(BF16))

    for u in units:
        d = pre[u]
        res = lax.dot_general(jnp.concatenate([d["kb"], d["q"]], axis=0).astype(BF16), d["k"], nt,
                              preferred_element_type=F32)
        d["low"] = jnp.where(strict, res[:CHUNK] * d["decay"], 0.0)
        d["attn"] = (res[CHUNK:] * d["decay"]).astype(BF16)
    for u in units:
        d = pre[u]
        lb16 = d["low"].astype(BF16)
        d["n"] = dot(lb16, lb16)
        d["p"] = eye - d["low"]
    for r in range(1, 6):
        for u in units:
            d = pre[u]
            nb16 = d["n"].astype(BF16)
            if r < 5:
                res = dot(jnp.concatenate([d["n"], d["p"]], axis=0).astype(BF16), nb16)
                d["n"] = res[:CHUNK]
                d["p"] = d["p"] + res[CHUNK:]
            else:
                d["p"] = d["p"] + dot(d["p"].astype(BF16), nb16)
    for u in units:
        d = pre[u]
        d["sol"] = dot(d["p"].astype(BF16), d["rhs"])

    st = {h: s_ref[h] for h in range(H_C)}
    for c in order:
        rows = slice(c * CHUNK, (c + 1) * CHUNK)
        res, vb = {}, {}
        for h in range(H_C):
            d = pre[c, h]
            lhs = jnp.concatenate([d["sol"][:, DV_C:], d["q_dec"]], axis=0).astype(BF16)
            res[h] = dot(lhs, st[h].astype(BF16))
        for h in range(H_C):
            vb[h] = (pre[c, h]["sol"][:, :DV_C] - res[h][:CHUNK]).astype(BF16)
        for h in range(H_C):
            d = pre[c, h]
            lanes = slice(h * HEAD_DIM, (h + 1) * HEAD_DIM)
            o_ref[0, rows, lanes] = (res[h][CHUNK:] + dot(d["attn"], vb[h])).astype(o_ref.dtype)
            st[h] = st[h] * jnp.exp(d["glast"]) + lax.dot_general(
                d["k_dec"], vb[h], (((0,), (0,)), ((), ())), preferred_element_type=F32)
    for h in range(H_C):
        s_ref[h] = st[h]


def _split_dot_left(a, b):
    hi = b.astype(BF16)
    lo = (b - hi.astype(F32)).astype(BF16)
    return (jnp.dot(a, hi, preferred_element_type=F32)
            + jnp.dot(a, lo, preferred_element_type=F32))


def _gdn_direction(qkv, small, a_log, dt_bias, rev, lb):
    bsz, t, _ = qkv.shape
    nb = t // lb
    blk = (lambda i: nb - 1 - i) if rev else (lambda i: i)
    off_a = 3 * H_C if rev else 2 * H_C
    pad = lambda vec: jnp.zeros((1, V7X_LANES), F32).at[0, off_a:off_a + H_C].set(vec)
    col = lambda j: pl.BlockSpec((1, lb, W_C), lambda b, i: (b, blk(i), j))
    vec = pl.BlockSpec((1, V7X_LANES), lambda b, i: (0, 0))
    return pl.pallas_call(
        functools.partial(_gdn_kernel, rev, lb),
        grid=(bsz, nb),
        in_specs=[col(0), col(1), col(2),
                  pl.BlockSpec((1, lb, V7X_LANES), lambda b, i: (b, blk(i), 0)),
                  vec, vec],
        out_specs=pl.BlockSpec((1, lb, W_C), lambda b, i: (b, blk(i), 0)),
        out_shape=jax.ShapeDtypeStruct((bsz, t, W_C), BF16),
        scratch_shapes=[pltpu.VMEM((H_C, DK_C, DV_C), F32)],
        compiler_params=_cparams(("parallel", "arbitrary")),
        name="gdn_bwd" if rev else "gdn_fwd",
    )(qkv, qkv, qkv, small, pad(a_log), pad(dt_bias))


def _gated_sum_kernel(x_ref, gn_ref, ma_ref, mb_ref, of_ref, ob_ref, gc_ref, og_ref,
                      wout_ref, wg0_ref, wg1_ref, wg2_ref, bg0_ref, bg1_ref, bg2_ref,
                      o_ref, xn_ref, mix_ref):
    j = pl.program_id(1)

    @pl.when(j == 0)
    def _():
        x = x_ref[...]
        xn = x * lax.rsqrt(jnp.mean(x * x, axis=-1, keepdims=True) + EPS) * gn_ref[...]
        xn_ref[...] = xn.astype(BF16)
        mix_ref[:, 0:W_A] = ma_ref[...]
        mix_ref[:, W_A:W_A + W_B] = mb_ref[...]
        for h in range(H_C):
            lanes = slice(h * HEAD_DIM, (h + 1) * HEAD_DIM)
            o = of_ref[:, lanes].astype(F32) + ob_ref[:, lanes].astype(F32)
            o = o * lax.rsqrt(jnp.mean(o * o, axis=-1, keepdims=True) + EPS) * og_ref[...]
            g = gc_ref[:, lanes].astype(F32)
            c0 = W_A + W_B + h * HEAD_DIM
            mix_ref[:, c0:c0 + HEAD_DIM] = (o * (g * jax.nn.sigmoid(g))).astype(BF16)

    xn = xn_ref[...]
    s = None
    for lo, hi, wg_ref, bg_ref in ((0, W_A, wg0_ref, bg0_ref),
                                   (W_A, W_A + W_B, wg1_ref, bg1_ref),
                                   (W_A + W_B, D_MODEL, wg2_ref, bg2_ref)):
        y = jnp.dot(mix_ref[:, lo:hi], wout_ref[lo:hi, :], preferred_element_type=F32)
        gate = jax.nn.sigmoid(jnp.dot(xn, wg_ref[...], preferred_element_type=F32) + bg_ref[...])
        s = gate * y if s is None else s + gate * y
    o_ref[...] = s.astype(o_ref.dtype)


def _gated_sum(x, norm_gain, mix_a, mix_b, o_f, o_b, main, o_gain, w_out, w_gate, b_gate, tm, tn):
    n, d = x.shape
    nj = d // tn
    cg = (COL_C + 3 * W_C) // W_C
    row = lambda w: pl.BlockSpec((tm, w), lambda i, j: (i, 0))
    wg = lambda br: pl.BlockSpec((d, tn), lambda i, j: (0, br * nj + j))
    bg = lambda br: pl.BlockSpec((1, tn), lambda i, j: (0, br * nj + j))
    return pl.pallas_call(
        _gated_sum_kernel,
        grid=(n // tm, nj),
        in_specs=[row(d),
                  pl.BlockSpec((1, d), lambda i, j: (0, 0)),
                  row(W_A), row(W_B), row(W_C), row(W_C),
                  pl.BlockSpec((tm, W_C), lambda i, j: (i, cg)),
                  pl.BlockSpec((1, HEAD_DIM), lambda i, j: (0, 0)),
                  pl.BlockSpec((d, tn), lambda i, j: (0, j)),
                  wg(0), wg(1), wg(2), bg(0), bg(1), bg(2)],
        out_specs=pl.BlockSpec((tm, tn), lambda i, j: (i, j)),
        out_shape=jax.ShapeDtypeStruct((n, d), BF16),
        scratch_shapes=[pltpu.VMEM((tm, d), BF16), pltpu.VMEM((tm, d), BF16)],
        compiler_params=_cparams(("parallel", "arbitrary"), V7X_VMEM_LIMIT_BYTES),
        name="gated_sum",
    )(x, norm_gain.reshape(1, d), mix_a, mix_b, o_f, o_b, main, o_gain.reshape(1, HEAD_DIM),
      w_out, w_gate, w_gate, w_gate, b_gate, b_gate, b_gate)


def _out_proj_kernel(x_ref, s_ref, w_ref, o_ref):
    o_ref[...] = x_ref[...] + jnp.dot(s_ref[...], w_ref[...], preferred_element_type=F32)


def _out_proj(x, s, w_o, tm):
    n, d = x.shape
    row = pl.BlockSpec((tm, d), lambda i: (i, 0))
    return pl.pallas_call(
        _out_proj_kernel,
        grid=(n // tm,),
        in_specs=[row, row, pl.BlockSpec((d, d), lambda i: (0, 0))],
        out_specs=row,
        out_shape=jax.ShapeDtypeStruct((n, d), F32),
        compiler_params=_cparams(("parallel",), V7X_VMEM_LIMIT_BYTES),
        name="out_proj",
    )(x, s, w_o)


def _router_kernel(x_ref, g_ref, wh_ref, wl_ref, h_ref, aff_ref):
    x = x_ref[...]
    h = x * lax.rsqrt(jnp.mean(x * x, axis=-1, keepdims=True) + EPS) * g_ref[...]
    h_ref[...] = h
    hh = h.astype(BF16)
    hl = (h - hh.astype(F32)).astype(BF16)
    logits = (jnp.dot(hh, wh_ref[...], preferred_element_type=F32)
              + jnp.dot(hl, wh_ref[...], preferred_element_type=F32)
              + jnp.dot(hh, wl_ref[...], preferred_element_type=F32))
    lane = lax.broadcasted_iota(I32, logits.shape, 1)
    valid = lane < N_EXPERTS
    logits = jnp.where(valid, logits, -jnp.inf)
    m = jnp.max(logits, axis=-1, keepdims=True)
    e = jnp.where(valid, jnp.exp(logits - m), 0.0)
    aff_ref[...] = e / jnp.sum(e, axis=-1, keepdims=True)


def _router(x, gain, w_router, tm):
    n, d = x.shape
    wpad = jnp.zeros((d, V7X_LANES), F32).at[:, :N_EXPERTS].set(w_router)
    w_hi = wpad.astype(BF16)
    w_lo = (wpad - w_hi.astype(F32)).astype(BF16)
    wspec = pl.BlockSpec((d, V7X_LANES), lambda i: (0, 0))
    return pl.pallas_call(
        _router_kernel,
        grid=(n // tm,),
        in_specs=[pl.BlockSpec((tm, d), lambda i: (i, 0)),
                  pl.BlockSpec((1, d), lambda i: (0, 0)), wspec, wspec],
        out_specs=[pl.BlockSpec((tm, d), lambda i: (i, 0)),
                   pl.BlockSpec((tm, V7X_LANES), lambda i: (i, 0))],
        out_shape=[jax.ShapeDtypeStruct((n, d), F32),
                   jax.ShapeDtypeStruct((n, V7X_LANES), F32)],
        compiler_params=_cparams(("parallel",), V7X_VMEM_LIMIT_BYTES),
        name="router",
    )(x, gain.reshape(1, d), w_hi, w_lo)


CUM_BLK = 256


def _select_kernel(cap, aff_ref, idx_ref, pos_ref, wsel_ref, starts_ref, cum_ref):
    t = aff_ref.shape[1]
    aff = aff_ref[0]

    def search(i, lo):
        cand = lo | jnp.left_shift(jnp.int32(1), 30 - i)
        cnt = jnp.sum((aff >= pltpu.bitcast(cand, F32)).astype(I32), axis=0, keepdims=True)
        return jnp.where(cnt >= cap, cand, lo)

    thr_bits = lax.fori_loop(0, 31, search, jnp.zeros((1, V7X_LANES), I32))
    thr = pltpu.bitcast(thr_bits, F32)
    gt = aff > thr
    eq = aff == thr
    need = (cap - jnp.sum(gt.astype(I32), axis=0, keepdims=True)).astype(F32)

    ri = lax.broadcasted_iota(I32, (CUM_BLK, CUM_BLK), 0)
    ci = lax.broadcasted_iota(I32, (CUM_BLK, CUM_BLK), 1)
    tri = (ri >= ci).astype(BF16)

    def cumsum_rows(mask_f32_of_block, store):
        off = jnp.zeros((1, V7X_LANES), F32)
        for blk in range(t // CUM_BLK):
            rows = slice(blk * CUM_BLK, (blk + 1) * CUM_BLK)
            c = jnp.dot(tri, mask_f32_of_block(rows).astype(BF16), preferred_element_type=F32) + off
            store(rows, c)
            off = c[CUM_BLK - 1:CUM_BLK, :]

    def store_cum(rows, c):
        cum_ref[rows, :] = c

    cumsum_rows(lambda rows: eq[rows].astype(F32), store_cum)
    sel = gt | (eq & (cum_ref[...] <= need))
    self32 = sel.astype(F32)
    cumsum_rows(lambda rows: self32[rows], store_cum)
    cum = cum_ref[...]
    pos_ref[0] = jnp.where(sel, cum.astype(I32) - 1, NEG_POS)
    wsel_ref[0] = jnp.where(sel, aff, 0.0)

    nstart = starts_ref.shape[1]
    tblk = t // (nstart - 1)
    starts_ref[0, 0:1, :] = jnp.zeros((1, V7X_LANES), I32)
    for tb in range(1, nstart):
        starts_ref[0, tb:tb + 1, :] = cum[tb * tblk - 1:tb * tblk, :].astype(I32)

    jj = lax.broadcasted_iota(I32, (1, cap), 1).astype(F32)
    ones = jnp.ones((8, CUM_BLK), BF16)
    for e in range(N_EXPERTS):
        acc = jnp.zeros((8, cap), F32)
        for blk in range(t // CUM_BLK):
            col = cum_ref[blk * CUM_BLK:(blk + 1) * CUM_BLK, e:e + 1]
            acc = acc + jnp.dot(ones, (col <= jj).astype(BF16), preferred_element_type=F32)
        idx_ref[0, e:e + 1, :] = acc[0:1, :].astype(I32)


def _select(aff, bsz, t, n_tok_blk):
    cap = CAPACITY_FACTOR * t // N_EXPERTS
    aff3 = aff.reshape(bsz, t, V7X_LANES)
    tl = pl.BlockSpec((1, t, V7X_LANES), lambda b: (b, 0, 0))
    return pl.pallas_call(
        functools.partial(_select_kernel, cap),
        grid=(bsz,),
        in_specs=[tl],
        out_specs=[pl.BlockSpec((1, N_EXPERTS, cap), lambda b: (b, 0, 0)), tl, tl,
                   pl.BlockSpec((1, n_tok_blk + 1, V7X_LANES), lambda b: (b, 0, 0))],
        out_shape=[jax.ShapeDtypeStruct((bsz, N_EXPERTS, cap), I32),
                   jax.ShapeDtypeStruct((bsz, t, V7X_LANES), I32),
                   jax.ShapeDtypeStruct((bsz, t, V7X_LANES), F32),
                   jax.ShapeDtypeStruct((bsz, n_tok_blk + 1, V7X_LANES), I32)],
        scratch_shapes=[pltpu.VMEM((t, V7X_LANES), F32)],
        compiler_params=_cparams(("parallel",), V7X_VMEM_LIMIT_BYTES),
        name="moe_select",
    )(aff3)


def _ffn_kernel(nf, nd, nbg, cap, cur_ref, nxt_ref, h_hbm, wg_ref, wu_ref, wd_ref, y_ref,
                xin, xb, hid, sem):
    s = pl.program_id(2)
    group = pl.program_id(0) * pl.num_programs(1) + pl.program_id(1)
    ngroups = pl.num_programs(0) * pl.num_programs(1)
    rows = nbg * cap
    per = rows // (nf + nd)
    tf = EXPERT_FF // nf

    def row_copy(idx_ref, j):
        return pltpu.make_async_copy(h_hbm.at[pl.ds(idx_ref[0, 0, j], 1), :],
                                     xin.at[pl.ds(j, 1), :], sem)

    def start_rows(idx_ref, base, n):
        def body(j, c):
            row_copy(idx_ref, base + j).start()
            return c
        lax.fori_loop(0, n, body, 0, unroll=8)

    def wait_rows(idx_ref, n):
        def body(j, c):
            row_copy(idx_ref, j).wait()
            return c
        lax.fori_loop(0, n, body, 0, unroll=8)

    @pl.when(s == 0)
    def _():
        @pl.when(group == 0)
        def _():
            start_rows(cur_ref, 0, rows)
        wait_rows(cur_ref, rows)
        xb[...] = xin[...].astype(BF16)

    @pl.when(group + 1 < ngroups)
    def _():
        start_rows(nxt_ref, s * per, per)

    @pl.when(s < nf)
    def _():
        x = xb[...]
        g = jnp.dot(x, wg_ref[0].astype(BF16), preferred_element_type=F32)
        u = jnp.dot(x, wu_ref[0].astype(BF16), preferred_element_type=F32)
        hid[s] = (g * jax.nn.sigmoid(g) * u).astype(BF16)

    @pl.when(s >= nf)
    def _():
        w = wd_ref[0].astype(BF16)
        acc = None
        for f in range(nf):
            part = jnp.dot(hid[f], w[f * tf:(f + 1) * tf, :], preferred_element_type=F32)
            acc = part if acc is None else acc + part
        for bi in range(nbg):
            y_ref[bi, 0] = acc[bi * cap:(bi + 1) * cap, :].astype(y_ref.dtype)


def _expert_ffn(idx, h, w_gate, w_up, w_down, layer, bsz, t, nf, nd):
    cap = idx.shape[-1]
    d = h.shape[1]
    nbg = 2 if bsz % 2 == 0 else 1
    nh = bsz // nbg
    rows = nbg * cap
    tf, tn = EXPERT_FF // nf, d // nd
    gidx = idx + (jnp.arange(bsz, dtype=I32) * t)[:, None, None]
    gidx = gidx.reshape(nh, nbg, N_EXPERTS, cap).transpose(2, 0, 1, 3).reshape(N_EXPERTS * nh, 1, rows)
    last = N_EXPERTS * nh - 1
    return pl.pallas_call(
        functools.partial(_ffn_kernel, nf, nd, nbg, cap),
        grid=(N_EXPERTS, nh, nf + nd),
        in_specs=[pl.BlockSpec((1, 1, rows), lambda e, hf, s: (e * nh + hf, 0, 0),
                               memory_space=pltpu.SMEM),
                  pl.BlockSpec((1, 1, rows), lambda e, hf, s: (jnp.minimum(e * nh + hf + 1, last), 0, 0),
                               memory_space=pltpu.SMEM),
                  pl.BlockSpec(memory_space=pl.ANY),
                  pl.BlockSpec((None, 1, d, tf), lambda e, hf, s: (layer, e, 0, jnp.minimum(s, nf - 1))),
                  pl.BlockSpec((None, 1, d, tf), lambda e, hf, s: (layer, e, 0, jnp.minimum(s, nf - 1))),
                  pl.BlockSpec((None, 1, EXPERT_FF, tn),
                               lambda e, hf, s: (layer, e, 0, jnp.maximum(s - nf, 0)))],
        out_specs=pl.BlockSpec((nbg, 1, cap, tn), lambda e, hf, s: (hf, e, 0, jnp.maximum(s - nf, 0))),
        out_shape=jax.ShapeDtypeStruct((bsz, N_EXPERTS, cap, d), BF16),
        scratch_shapes=[pltpu.VMEM((rows, d), F32), pltpu.VMEM((rows, d), BF16),
                        pltpu.VMEM((nf, rows, tf), BF16), pltpu.SemaphoreType.DMA],
        compiler_params=_cparams(("arbitrary", "arbitrary", "arbitrary"), V7X_VMEM_LIMIT_BYTES),
        name="moe_ffn",
    )(gidx, gidx, h, w_gate, w_up, w_down)


def _combine_kernel(cap, tblk, starts_ref, x_ref, pos_ref, w_ref, y_hbm, o_ref, kbuf, sem):
    b = pl.program_id(0)
    tb = pl.program_id(1)
    o_ref[0] = x_ref[0]

    first, npass = [], jnp.int32(0)
    for e in range(N_EXPERTS):
        s0 = starts_ref[0, tb, e]
        s1 = starts_ref[0, tb + 1, e]
        al = (s0 // 16) * 16
        first.append(al)
        npass = jnp.maximum(npass, jnp.where(s1 > s0, (s1 - al + ROW_WIN - 1) // ROW_WIN, 0))

    def window_copy(e, off):
        return pltpu.make_async_copy(y_hbm.at[b, e, pl.ds(off, ROW_WIN), :],
                                     kbuf.at[pl.ds(e * ROW_WIN, ROW_WIN), :], sem)

    def one_pass(p, carry):
        lows = [first[e] + p * ROW_WIN for e in range(N_EXPERTS)]
        offs = [pl.multiple_of(jnp.minimum(lo, cap - ROW_WIN), 16) for lo in lows]
        for e in range(N_EXPERTS):
            window_copy(e, offs[e]).start()
        lane = lax.broadcasted_iota(I32, (tblk, V7X_LANES), 1)
        slot = lane % ROW_WIN
        head = lane < ROW_WIN
        tiles = []
        for g in range(N_EXPERTS * ROW_WIN // V7X_LANES):
            ea, eb = 2 * g, 2 * g + 1
            pe = jnp.where(head, pos_ref[0, :, ea:ea + 1], pos_ref[0, :, eb:eb + 1])
            we = jnp.where(head, w_ref[0, :, ea:ea + 1], w_ref[0, :, eb:eb + 1])
            off = jnp.where(head, offs[ea], offs[eb])
            low = jnp.where(head, lows[ea], lows[eb])
            hit = (pe == off + slot) & (pe >= low)
            tiles.append(jnp.where(hit, we, 0.0).astype(BF16))
        pmat = jnp.concatenate(tiles, axis=1)
        for e in range(N_EXPERTS):
            window_copy(e, offs[e]).wait()
        o_ref[0] += jnp.dot(pmat, kbuf[...], preferred_element_type=F32)
        return carry

    lax.fori_loop(0, npass, one_pass, 0)


def _combine(x, pos, wsel, starts, y, tblk):
    bsz, t, d = x.shape
    cap = y.shape[2]
    return pl.pallas_call(
        functools.partial(_combine_kernel, cap, tblk),
        grid=(bsz, t // tblk),
        in_specs=[pl.BlockSpec((1, t // tblk + 1, V7X_LANES), lambda b, i: (b, 0, 0),
                               memory_space=pltpu.SMEM),
                  pl.BlockSpec((1, tblk, d), lambda b, i: (b, i, 0)),
                  pl.BlockSpec((1, tblk, V7X_LANES), lambda b, i: (b, i, 0)),
                  pl.BlockSpec((1, tblk, V7X_LANES), lambda b, i: (b, i, 0)),
                  pl.BlockSpec(memory_space=pl.ANY)],
        out_specs=pl.BlockSpec((1, tblk, d), lambda b, i: (b, i, 0)),
        out_shape=jax.ShapeDtypeStruct((bsz, t, d), F32),
        scratch_shapes=[pltpu.VMEM((N_EXPERTS * ROW_WIN, d), BF16), pltpu.SemaphoreType.DMA],
        compiler_params=_cparams(("parallel", "arbitrary")),
        name="moe_combine",
    )(starts, x, pos, wsel, y)


def _tiles(n, t):
    pick = lambda want, total: want if total % want == 0 else total
    return dict(
        tm_proj=pick(1024, n), tn_proj=768,
        tq=pick(512, t), tk=pick(512, t), lb=pick(512, t),
        tm_gsum=pick(1024, n), tn_gsum=256, tm_oproj=pick(512, n),
        tm_router=pick(512, n),
        nf_ffn=4, nd_ffn=4,
        tblk=pick(256, t),
    )


def kernel(x, positions, norm_mix, w_in, conv_a, q_norm, k_norm, lambda_q1, lambda_k1, lambda_q2, lambda_k2, subln, conv_c, a_log_f, a_log_b, dt_bias_f, dt_bias_b, o_norm, w_out_a, w_out_b, w_out_c, w_gate, b_gate, w_o, norm_ffn, w_router, w_e_gate, w_e_up, w_e_down):
    bsz, t, d = x.shape
    n = bsz * t
    depth = w_in.shape[0]
    ts = _tiles(n, t)
    tables = _rope_tables(positions)
    pad_small = jnp.zeros((d, V7X_LANES - N_SMALL), F32)
    tile2 = lambda g: jnp.concatenate([g, g]).reshape(1, V7X_LANES)
    pad64 = lambda vv: jnp.concatenate([vv, jnp.zeros_like(vv)])

    xf = x.reshape(n, d)
    for l in range(depth):
        lam_init = 0.8 - 0.6 * math.exp(-0.3 * l)
        w_small = jnp.concatenate([w_in[l, :, N_MAIN:], pad_small], axis=1)[None]
        main = _norm_matmul(xf, norm_mix[l], w_in, l, N_MAIN, BF16, ts["tm_proj"], ts["tn_proj"])
        small = _norm_matmul(xf, norm_mix[l], w_small, 0, V7X_LANES, F32, ts["tm_proj"], V7X_LANES)
        main3 = main.reshape(bsz, t, N_MAIN)
        small3 = small.reshape(bsz, t, V7X_LANES)

        mix_a = _mixer_a(main3, conv_a[l])
        lamv = jnp.stack([pad64(lambda_q1[l]), pad64(lambda_k1[l]),
                          pad64(lambda_q2[l]), pad64(lambda_k2[l])])
        mix_b = _diff_attention(main3, tables, tile2(q_norm[l]), tile2(k_norm[l]), lamv,
                                subln[l].reshape(1, V7X_LANES), lam_init, ts["tq"], ts["tk"])
        qkv_c = _gdn_prep(main3, conv_c[l])
        o_f = _gdn_direction(qkv_c, small3, a_log_f[l], dt_bias_f[l], False, ts["lb"])
        o_b = _gdn_direction(qkv_c, small3, a_log_b[l], dt_bias_b[l], True, ts["lb"])

        w_out = jnp.concatenate([w_out_a[l], w_out_b[l], w_out_c[l]], axis=0).astype(BF16)
        gsum = _gated_sum(xf, norm_mix[l], mix_a.reshape(n, W_A), mix_b.reshape(n, W_B),
                          o_f.reshape(n, W_C), o_b.reshape(n, W_C), main, o_norm[l],
                          w_out, w_gate[l].astype(BF16), b_gate[l].reshape(1, N_BRANCH * d),
                          ts["tm_gsum"], ts["tn_gsum"])
        x1 = _out_proj(xf, gsum, w_o[l].astype(BF16), ts["tm_oproj"])

        h, aff = _router(x1, norm_ffn[l], w_router[l], ts["tm_router"])
        idx, pos, wsel, starts = _select(aff, bsz, t, t // ts["tblk"])
        y = _expert_ffn(idx, h, w_e_gate, w_e_up, w_e_down, l, bsz, t, ts["nf_ffn"], ts["nd_ffn"])
        xf = _combine(x1.reshape(bsz, t, d), pos, wsel, starts, y, ts["tblk"]).reshape(n, d)
    return xf.reshape(bsz, t, d)
```

```python
import functools
import math

import jax
import jax.numpy as jnp
from jax import lax
from jax.experimental import pallas as pl
from jax.experimental.pallas import tpu as pltpu

F32 = jnp.float32
BF16 = jnp.bfloat16
I32 = jnp.int32

D_MODEL = 2048
HEAD_DIM = 128
W_A = D_MODEL // 4
DH_B = 64
DV_B = 2 * DH_B
H_B = (3 * D_MODEL // 8) // DV_B
W_B = H_B * DV_B
ROT_DIM = DH_B // 4
ROPE_THETA = 500000.0
DK_C = HEAD_DIM
DV_C = HEAD_DIM
H_C = (3 * D_MODEL // 8) // DV_C
W_C = H_C * DV_C
CHUNK = 64
N_BRANCH = 3
N_EXPERTS = 16
CAPACITY_FACTOR = 2
EXPERT_FF = D_MODEL // 2
EPS = 1e-6

N_MAIN = 3 * W_A + 3 * W_B + 4 * W_C
N_SMALL = 4 * H_C
COL_A = 0
COL_B = 3 * W_A
COL_C = COL_B + 3 * W_B

V7X_LANES = 128
V7X_VMEM_LIMIT_BYTES = 56 * 1024 * 1024
ROW_WIN = 64
NEG_POS = -(1 << 20)


def _cparams(sem, vmem=None):
    return pltpu.CompilerParams(dimension_semantics=sem, vmem_limit_bytes=vmem)


def _split_dot(a, b):
    hi = a.astype(BF16)
    lo = (a - hi.astype(F32)).astype(BF16)
    return (jnp.dot(hi, b, preferred_element_type=F32)
            + jnp.dot(lo, b, preferred_element_type=F32))


def _norm_matmul_kernel(x_ref, g_ref, w_ref, o_ref, xn_ref):
    @pl.when(pl.program_id(1) == 0)
    def _():
        x = x_ref[...]
        xn = x * lax.rsqrt(jnp.mean(x * x, axis=-1, keepdims=True) + EPS) * g_ref[...]
        xn_ref[...] = xn.astype(BF16)

    o_ref[...] = jnp.dot(xn_ref[...], w_ref[...].astype(BF16),
                         preferred_element_type=F32).astype(o_ref.dtype)


def _norm_matmul(x, gain, w, layer, nout, out_dtype, tm, tn):
    n, d = x.shape
    return pl.pallas_call(
        _norm_matmul_kernel,
        grid=(n // tm, nout // tn),
        in_specs=[pl.BlockSpec((tm, d), lambda i, j: (i, 0)),
                  pl.BlockSpec((1, d), lambda i, j: (0, 0)),
                  pl.BlockSpec((None, d, tn), lambda i, j: (layer, 0, j))],
        out_specs=pl.BlockSpec((tm, tn), lambda i, j: (i, j)),
        out_shape=jax.ShapeDtypeStruct((n, nout), out_dtype),
        scratch_shapes=[pltpu.VMEM((tm, d), BF16)],
        compiler_params=_cparams(("parallel", "arbitrary"), V7X_VMEM_LIMIT_BYTES),
        name="norm_matmul",
    )(x, gain.reshape(1, d), w)


def _conv3(x, w):
    t = x.shape[0]
    row = lax.broadcasted_iota(I32, x.shape, 0)
    prev = jnp.where(row == 0, 0.0, pltpu.roll(x, 1, 0))
    nxt = jnp.where(row == t - 1, 0.0, pltpu.roll(x, t - 1, 0))
    return prev * w[0:1, :] + x * w[1:2, :] + nxt * w[2:3, :]


def _mixer_a_kernel(b_ref, c_ref, v_ref, w_ref, o_ref):
    cv = c_ref[0].astype(F32) * v_ref[0].astype(F32)
    o_ref[0] = (b_ref[0].astype(F32) * _conv3(cv, w_ref[...])).astype(o_ref.dtype)


def _mixer_a(main, conv_w):
    bsz, t, _ = main.shape
    nblk = W_A // V7X_LANES
    c0 = COL_A // V7X_LANES
    spec = lambda off: pl.BlockSpec((1, t, V7X_LANES), lambda b, c: (b, 0, c0 + off + c))
    return pl.pallas_call(
        _mixer_a_kernel,
        grid=(bsz, nblk),
        in_specs=[spec(0), spec(nblk), spec(2 * nblk),
                  pl.BlockSpec((3, V7X_LANES), lambda b, c: (0, c))],
        out_specs=pl.BlockSpec((1, t, V7X_LANES), lambda b, c: (b, 0, c)),
        out_shape=jax.ShapeDtypeStruct((bsz, t, W_A), BF16),
        compiler_params=_cparams(("parallel", "parallel")),
        name="mixer_a",
    )(main, main, main, conv_w)


def _gdn_prep_kernel(x_ref, w_ref, o_ref):
    c = pl.program_id(1)
    y = _conv3(x_ref[0].astype(F32), w_ref[...])
    y = y * jax.nn.sigmoid(y)
    inv = lax.rsqrt(jnp.sum(y * y, axis=-1, keepdims=True) + EPS)
    scale = jnp.where(c < H_C, inv * (DK_C ** -0.5), jnp.where(c < 2 * H_C, inv, 1.0))
    o_ref[0] = (y * scale).astype(o_ref.dtype)


def _gdn_prep(main, conv_w):
    bsz, t, _ = main.shape
    nblk = 3 * W_C // V7X_LANES
    c0 = COL_C // V7X_LANES
    return pl.pallas_call(
        _gdn_prep_kernel,
        grid=(bsz, nblk),
        in_specs=[pl.BlockSpec((1, t, V7X_LANES), lambda b, c: (b, 0, c0 + c)),
                  pl.BlockSpec((3, V7X_LANES), lambda b, c: (0, c))],
        out_specs=pl.BlockSpec((1, t, V7X_LANES), lambda b, c: (b, 0, c)),
        out_shape=jax.ShapeDtypeStruct((bsz, t, 3 * W_C), BF16),
        compiler_params=_cparams(("parallel", "parallel")),
        name="gdn_prep",
    )(main, conv_w)


def _rope_kernel(pos_ref, c_ref, s1_ref, s2_ref):
    pos = pos_ref[0].astype(F32)
    lane = lax.broadcasted_iota(I32, (1, V7X_LANES), 1)
    l64 = lane % DH_B
    half = ROT_DIM // 2
    fidx = jnp.where(l64 < half, l64, l64 - half).astype(F32)
    inv_freq = jnp.exp(fidx * (-math.log(ROPE_THETA) * 2.0 / ROT_DIM))
    ang = pos * inv_freq
    cos, sin = jnp.cos(ang), jnp.sin(ang)
    first = l64 < half
    second = (l64 >= half) & (l64 < ROT_DIM)
    c_ref[0] = jnp.where(first | second, cos, 1.0)
    s1_ref[0] = jnp.where(first, -sin, 0.0)
    s2_ref[0] = jnp.where(second, sin, 0.0)


def _rope_tables(positions):
    bsz, t = positions.shape
    shp = jax.ShapeDtypeStruct((bsz, t, V7X_LANES), F32)
    spec = pl.BlockSpec((1, t, V7X_LANES), lambda b: (b, 0, 0))
    return pl.pallas_call(
        _rope_kernel,
        grid=(bsz,),
        in_specs=[pl.BlockSpec((1, t, 1), lambda b: (b, 0, 0))],
        out_specs=[spec, spec, spec],
        out_shape=[shp, shp, shp],
        compiler_params=_cparams(("parallel",)),
        name="rope_tables",
    )(positions.reshape(bsz, t, 1))


def _qk_prep(x, gain, cos, s1, s2):
    r = lax.broadcasted_iota(I32, (V7X_LANES, V7X_LANES), 0) // DH_B
    c = lax.broadcasted_iota(I32, (V7X_LANES, V7X_LANES), 1) // DH_B
    blockdiag = (r == c).astype(BF16)
    ssq = _split_dot(x * x, blockdiag)
    xn = x * lax.rsqrt(ssq * (1.0 / DH_B) + EPS) * gain
    half = ROT_DIM // 2
    return xn * cos + pltpu.roll(xn, V7X_LANES - half, 1) * s1 + pltpu.roll(xn, half, 1) * s2


def _attn_kernel(lam_init, tq, tk, q_ref, k_ref, v_ref, c_ref, s1_ref, s2_ref,
                 qg_ref, kg_ref, lamv_ref, sub_ref, o_ref, kp_ref, vx_ref, acc_ref):
    qi = pl.program_id(2)
    t = kp_ref.shape[0]

    @pl.when(qi == 0)
    def _():
        kp_ref[...] = _qk_prep(k_ref[0].astype(F32), kg_ref[...],
                               c_ref[0], s1_ref[0], s2_ref[0]).astype(BF16)
        vx_ref[:, :V7X_LANES] = v_ref[0]
        vx_ref[:, V7X_LANES:] = jnp.ones((t, V7X_LANES), BF16)

    rows = pl.ds(pl.multiple_of(qi * tq, tq), tq)
    q = _qk_prep(q_ref[0].astype(F32), qg_ref[...], c_ref[0, rows, :], s1_ref[0, rows, :],
                 s2_ref[0, rows, :]) * (DH_B ** -0.5 * math.log2(math.e))
    lane = lax.broadcasted_iota(I32, (tq, V7X_LANES), 1)
    qsub = (jnp.where(lane < DH_B, q, 0.0).astype(BF16), jnp.where(lane >= DH_B, q, 0.0).astype(BF16))

    def scores(j):
        kj = kp_ref[j * tk:(j + 1) * tk, :]
        return [lax.dot_general(qc, kj, (((1,), (1,)), ((), ())), preferred_element_type=F32)
                for qc in qsub]

    acc_ref[...] = jnp.zeros_like(acc_ref)
    m = [jnp.full((tq, 1), -jnp.inf, F32)] * 2
    nkb = t // tk
    ahead = 2
    pending = [scores(j) for j in range(min(ahead, nkb))]
    for j in range(nkb):
        cur = pending.pop(0)
        if j + ahead < nkb:
            pending.append(scores(j + ahead))
        vj = vx_ref[j * tk:(j + 1) * tk, :]
        for c in range(2):
            m_new = jnp.maximum(m[c], jnp.max(cur[c], axis=-1, keepdims=True))
            e = jnp.exp2(cur[c] - m_new).astype(BF16)
            acc_ref[c] = acc_ref[c] * jnp.exp2(m[c] - m_new) + jnp.dot(e, vj, preferred_element_type=F32)
            m[c] = m_new
    pv = [acc_ref[c, :, :V7X_LANES] for c in range(2)]
    l = [acc_ref[c, :, V7X_LANES:V7X_LANES + 1] for c in range(2)]
    lv = lamv_ref[...]
    lam = (jnp.exp(jnp.sum(lv[0:1] * lv[1:2], axis=-1, keepdims=True))
           - jnp.exp(jnp.sum(lv[2:3] * lv[3:4], axis=-1, keepdims=True)) + lam_init)
    o = pv[0] / l[0] - lam * (pv[1] / l[1])
    o = o * lax.rsqrt(jnp.mean(o * o, axis=-1, keepdims=True) + EPS) * sub_ref[...]
    o_ref[0] = (o * (1.0 - lam_init)).astype(o_ref.dtype)


def _diff_attention(main, tables, q_gain, k_gain, lamv, subln, lam_init, tq, tk):
    bsz, t, _ = main.shape
    cq = COL_B // V7X_LANES
    ck = cq + H_B
    cv = ck + H_B
    tab = pl.BlockSpec((1, t, V7X_LANES), lambda b, h, i: (b, 0, 0))
    vec = pl.BlockSpec((1, V7X_LANES), lambda b, h, i: (0, 0))
    return pl.pallas_call(
        functools.partial(_attn_kernel, lam_init, tq, tk),
        grid=(bsz, H_B, t // tq),
        in_specs=[pl.BlockSpec((1, tq, V7X_LANES), lambda b, h, i: (b, i, cq + h)),
                  pl.BlockSpec((1, t, V7X_LANES), lambda b, h, i: (b, 0, ck + h)),
                  pl.BlockSpec((1, t, V7X_LANES), lambda b, h, i: (b, 0, cv + h)),
                  tab, tab, tab, vec, vec,
                  pl.BlockSpec((4, V7X_LANES), lambda b, h, i: (0, 0)),
                  vec],
        out_specs=pl.BlockSpec((1, tq, V7X_LANES), lambda b, h, i: (b, i, h)),
        out_shape=jax.ShapeDtypeStruct((bsz, t, W_B), BF16),
        scratch_shapes=[pltpu.VMEM((t, V7X_LANES), BF16),
                        pltpu.VMEM((t, 2 * V7X_LANES), BF16),
                        pltpu.VMEM((2, tq, 2 * V7X_LANES), F32)],
        compiler_params=_cparams(("parallel", "parallel", "arbitrary"), V7X_VMEM_LIMIT_BYTES),
        name="diff_attention",
    )(main, main, main, *tables, q_gain, k_gain, lamv, subln)


def _gdn_kernel(rev, lb, q_ref, k_ref, v_ref, sm_ref, alog_ref, dtb_ref, o_ref, s_ref):
    @pl.when(pl.program_id(1) == 0)
    def _():
        s_ref[...] = jnp.zeros_like(s_ref)

    off_b = H_C if rev else 0
    off_a = 3 * H_C if rev else 2 * H_C
    ri = lax.broadcasted_iota(I32, (CHUNK, CHUNK), 0)
    ci = lax.broadcasted_iota(I32, (CHUNK, CHUNK), 1)
    incl = (ri <= ci) if rev else (ri >= ci)
    strict = (ri < ci) if rev else (ri > ci)
    cum_mat = incl.astype(BF16)
    eye = (ri == ci).astype(F32)
    last = 0 if rev else CHUNK - 1

    nchunk = lb // CHUNK
    order = list(range(nchunk - 1, -1, -1) if rev else range(nchunk))
    units = [(c, h) for c in order for h in range(H_C)]
    nt = (((1,), (1,)), ((), ()))
    dot = functools.partial(jnp.dot, preferred_element_type=F32)

    gates = {}
    for c in order:
        rows = slice(c * CHUNK, (c + 1) * CHUNK)
        sm = sm_ref[0, rows, :]
        g_all = -jnp.exp(alog_ref[...]) * jax.nn.softplus(sm + dtb_ref[...])
        gc_all = _split_dot_left(cum_mat, g_all)
        gates[c] = (jax.nn.sigmoid(sm), gc_all, gc_all.T)

    pre = {}
    for c, h in units:
        rows = slice(c * CHUNK, (c + 1) * CHUNK)
        lanes = slice(h * HEAD_DIM, (h + 1) * HEAD_DIM)
        beta_all, gc_all, gct_all = gates[c]
        q = q_ref[0, rows, lanes].astype(F32)
        k = k_ref[0, rows, lanes]
        kf = k.astype(F32)
        beta = beta_all[:, off_b + h:off_b + h + 1]
        gcol = gc_all[:, off_a + h:off_a + h + 1]
        grow = gct_all[off_a + h:off_a + h + 1, :]
        glast = gcol[last:last + 1, :]
        decay = jnp.where(incl, jnp.exp(jnp.where(incl, gcol - grow, 0.0)), 0.0)
        kb = kf * beta
        eg = jnp.exp(gcol)
        rhs = jnp.concatenate([v_ref[0, rows, lanes].astype(F32) * beta, kb * eg], axis=-1).astype(BF16)
        pre[c, h] = dict(k=k, kb=kb, decay=decay, rhs=rhs, glast=glast,
                         q=q, q_dec=q * eg, k_dec=(kf * jnp.exp(glast - gcol)).astype(BF16))

    for u in units:
        d = pre[u]
        res = lax.dot_general(jnp.concatenate([d["kb"], d["q"]], axis=0).astype(BF16), d["k"], nt,
                              preferred_element_type=F32)
        d["low"] = jnp.where(strict, res[:CHUNK] * d["decay"], 0.0)
        d["attn"] = (res[CHUNK:] * d["decay"]).astype(BF16)
    for u in units:
        d = pre[u]
        lb16 = d["low"].astype(BF16)
        d["n"] = dot(lb16, lb16)
        d["p"] = eye - d["low"]
    for r in range(1, 6):
        for u in units:
            d = pre[u]
            nb16 = d["n"].astype(BF16)
            if r < 5:
                res = dot(jnp.concatenate([d["n"], d["p"]], axis=0).astype(BF16), nb16)
                d["n"] = res[:CHUNK]
                d["p"] = d["p"] + res[CHUNK:]
            else:
                d["p"] = d["p"] + dot(d["p"].astype(BF16), nb16)
    for u in units:
        d = pre[u]
        d["sol"] = dot(d["p"].astype(BF16), d["rhs"])

    st = {h: s_ref[h] for h in range(H_C)}
    for c in order:
        rows = slice(c * CHUNK, (c + 1) * CHUNK)
        res, vb = {}, {}
        for h in range(H_C):
            d = pre[c, h]
            lhs = jnp.concatenate([d["sol"][:, DV_C:], d["q_dec"]], axis=0).astype(BF16)
            res[h] = dot(lhs, st[h].astype(BF16))
        for h in range(H_C):
            vb[h] = (pre[c, h]["sol"][:, :DV_C] - res[h][:CHUNK]).astype(BF16)
        for h in range(H_C):
            d = pre[c, h]
            lanes = slice(h * HEAD_DIM, (h + 1) * HEAD_DIM)
            o_ref[0, rows, lanes] = (res[h][CHUNK:] + dot(d["attn"], vb[h])).astype(o_ref.dtype)
            st[h] = st[h] * jnp.exp(d["glast"]) + lax.dot_general(
                d["k_dec"], vb[h], (((0,), (0,)), ((), ())), preferred_element_type=F32)
    for h in range(H_C):
        s_ref[h] = st[h]


def _split_dot_left(a, b):
    hi = b.astype(BF16)
    lo = (b - hi.astype(F32)).astype(BF16)
    return (jnp.dot(a, hi, preferred_element_type=F32)
            + jnp.dot(a, lo, preferred_element_type=F32))


def _gdn_direction(qkv, small, a_log, dt_bias, rev, lb):
    bsz, t, _ = qkv.shape
    nb = t // lb
    blk = (lambda i: nb - 1 - i) if rev else (lambda i: i)
    off_a = 3 * H_C if rev else 2 * H_C
    pad = lambda vec: jnp.zeros((1, V7X_LANES), F32).at[0, off_a:off_a + H_C].set(vec)
    col = lambda j: pl.BlockSpec((1, lb, W_C), lambda b, i: (b, blk(i), j))
    vec = pl.BlockSpec((1, V7X_LANES), lambda b, i: (0, 0))
    return pl.pallas_call(
        functools.partial(_gdn_kernel, rev, lb),
        grid=(bsz, nb),
        in_specs=[col(0), col(1), col(2),
                  pl.BlockSpec((1, lb, V7X_LANES), lambda b, i: (b, blk(i), 0)),
                  vec, vec],
        out_specs=pl.BlockSpec((1, lb, W_C), lambda b, i: (b, blk(i), 0)),
        out_shape=jax.ShapeDtypeStruct((bsz, t, W_C), BF16),
        scratch_shapes=[pltpu.VMEM((H_C, DK_C, DV_C), F32)],
        compiler_params=_cparams(("parallel", "arbitrary")),
        name="gdn_bwd" if rev else "gdn_fwd",
    )(qkv, qkv, qkv, small, pad(a_log), pad(dt_bias))


def _gated_sum_kernel(x_ref, gn_ref, ma_ref, mb_ref, of_ref, ob_ref, gc_ref, og_ref,
                      wout_ref, wg0_ref, wg1_ref, wg2_ref, bg0_ref, bg1_ref, bg2_ref,
                      o_ref, xn_ref, mix_ref):
    j = pl.program_id(1)

    @pl.when(j == 0)
    def _():
        x = x_ref[...]
        xn = x * lax.rsqrt(jnp.mean(x * x, axis=-1, keepdims=True) + EPS) * gn_ref[...]
        xn_ref[...] = xn.astype(BF16)
        mix_ref[:, 0:W_A] = ma_ref[...]
        mix_ref[:, W_A:W_A + W_B] = mb_ref[...]
        for h in range(H_C):
            lanes = slice(h * HEAD_DIM, (h + 1) * HEAD_DIM)
            o = of_ref[:, lanes].astype(F32) + ob_ref[:, lanes].astype(F32)
            o = o * lax.rsqrt(jnp.mean(o * o, axis=-1, keepdims=True) + EPS) * og_ref[...]
            g = gc_ref[:, lanes].astype(F32)
            c0 = W_A + W_B + h * HEAD_DIM
            mix_ref[:, c0:c0 + HEAD_DIM] = (o * (g * jax.nn.sigmoid(g))).astype(BF16)

    xn = xn_ref[...]
    s = None
    for lo, hi, wg_ref, bg_ref in ((0, W_A, wg0_ref, bg0_ref),
                                   (W_A, W_A + W_B, wg1_ref, bg1_ref),
                                   (W_A + W_B, D_MODEL, wg2_ref, bg2_ref)):
        y = jnp.dot(mix_ref[:, lo:hi], wout_ref[lo:hi, :], preferred_element_type=F32)
        gate = jax.nn.sigmoid(jnp.dot(xn, wg_ref[...], preferred_element_type=F32) + bg_ref[...])
        s = gate * y if s is None else s + gate * y
    o_ref[...] = s.astype(o_ref.dtype)


def _gated_sum(x, norm_gain, mix_a, mix_b, o_f, o_b, main, o_gain, w_out, w_gate, b_gate, tm, tn):
    n, d = x.shape
    nj = d // tn
    cg = (COL_C + 3 * W_C) // W_C
    row = lambda w: pl.BlockSpec((tm, w), lambda i, j: (i, 0))
    wg = lambda br: pl.BlockSpec((d, tn), lambda i, j: (0, br * nj + j))
    bg = lambda br: pl.BlockSpec((1, tn), lambda i, j: (0, br * nj + j))
    return pl.pallas_call(
        _gated_sum_kernel,
        grid=(n // tm, nj),
        in_specs=[row(d),
                  pl.BlockSpec((1, d), lambda i, j: (0, 0)),
                  row(W_A), row(W_B), row(W_C), row(W_C),
                  pl.BlockSpec((tm, W_C), lambda i, j: (i, cg)),
                  pl.BlockSpec((1, HEAD_DIM), lambda i, j: (0, 0)),
                  pl.BlockSpec((d, tn), lambda i, j: (0, j)),
                  wg(0), wg(1), wg(2), bg(0), bg(1), bg(2)],
        out_specs=pl.BlockSpec((tm, tn), lambda i, j: (i, j)),
        out_shape=jax.ShapeDtypeStruct((n, d), BF16),
        scratch_shapes=[pltpu.VMEM((tm, d), BF16), pltpu.VMEM((tm, d), BF16)],
        compiler_params=_cparams(("parallel", "arbitrary"), V7X_VMEM_LIMIT_BYTES),
        name="gated_sum",
    )(x, norm_gain.reshape(1, d), mix_a, mix_b, o_f, o_b, main, o_gain.reshape(1, HEAD_DIM),
      w_out, w_gate, w_gate, w_gate, b_gate, b_gate, b_gate)


def _out_proj_kernel(x_ref, s_ref, w_ref, o_ref):
    o_ref[...] = x_ref[...] + jnp.dot(s_ref[...], w_ref[...], preferred_element_type=F32)


def _out_proj(x, s, w_o, tm):
    n, d = x.shape
    row = pl.BlockSpec((tm, d), lambda i: (i, 0))
    return pl.pallas_call(
        _out_proj_kernel,
        grid=(n // tm,),
        in_specs=[row, row, pl.BlockSpec((d, d), lambda i: (0, 0))],
        out_specs=row,
        out_shape=jax.ShapeDtypeStruct((n, d), F32),
        compiler_params=_cparams(("parallel",), V7X_VMEM_LIMIT_BYTES),
        name="out_proj",
    )(x, s, w_o)


def _router_kernel(x_ref, g_ref, wh_ref, wl_ref, h_ref, aff_ref):
    x = x_ref[...]
    h = x * lax.rsqrt(jnp.mean(x * x, axis=-1, keepdims=True) + EPS) * g_ref[...]
    h_ref[...] = h
    hh = h.astype(BF16)
    hl = (h - hh.astype(F32)).astype(BF16)
    logits = (jnp.dot(hh, wh_ref[...], preferred_element_type=F32)
              + jnp.dot(hl, wh_ref[...], preferred_element_type=F32)
              + jnp.dot(hh, wl_ref[...], preferred_element_type=F32))
    lane = lax.broadcasted_iota(I32, logits.shape, 1)
    valid = lane < N_EXPERTS
    logits = jnp.where(valid, logits, -jnp.inf)
    m = jnp.max(logits, axis=-1, keepdims=True)
    e = jnp.where(valid, jnp.exp(logits - m), 0.0)
    aff_ref[...] = e / jnp.sum(e, axis=-1, keepdims=True)


def _router(x, gain, w_router, tm):
    n, d = x.shape
    wpad = jnp.zeros((d, V7X_LANES), F32).at[:, :N_EXPERTS].set(w_router)
    w_hi = wpad.astype(BF16)
    w_lo = (wpad - w_hi.astype(F32)).astype(BF16)
    wspec = pl.BlockSpec((d, V7X_LANES), lambda i: (0, 0))
    return pl.pallas_call(
        _router_kernel,
        grid=(n // tm,),
        in_specs=[pl.BlockSpec((tm, d), lambda i: (i, 0)),
                  pl.BlockSpec((1, d), lambda i: (0, 0)), wspec, wspec],
        out_specs=[pl.BlockSpec((tm, d), lambda i: (i, 0)),
                   pl.BlockSpec((tm, V7X_LANES), lambda i: (i, 0))],
        out_shape=[jax.ShapeDtypeStruct((n, d), F32),
                   jax.ShapeDtypeStruct((n, V7X_LANES), F32)],
        compiler_params=_cparams(("parallel",), V7X_VMEM_LIMIT_BYTES),
        name="router",
    )(x, gain.reshape(1, d), w_hi, w_lo)


CUM_BLK = 256


def _select_kernel(cap, aff_ref, idx_ref, pos_ref, wsel_ref, starts_ref, cum_ref):
    t = aff_ref.shape[1]
    aff = aff_ref[0]

    def search(i, lo):
        cand = lo | jnp.left_shift(jnp.int32(1), 30 - i)
        cnt = jnp.sum((aff >= pltpu.bitcast(cand, F32)).astype(I32), axis=0, keepdims=True)
        return jnp.where(cnt >= cap, cand, lo)

    thr_bits = lax.fori_loop(0, 31, search, jnp.zeros((1, V7X_LANES), I32))
    thr = pltpu.bitcast(thr_bits, F32)
    gt = aff > thr
    eq = aff == thr
    need = (cap - jnp.sum(gt.astype(I32), axis=0, keepdims=True)).astype(F32)

    ri = lax.broadcasted_iota(I32, (CUM_BLK, CUM_BLK), 0)
    ci = lax.broadcasted_iota(I32, (CUM_BLK, CUM_BLK), 1)
    tri = (ri >= ci).astype(BF16)

    def cumsum_rows(mask_f32_of_block, store):
        off = jnp.zeros((1, V7X_LANES), F32)
        for blk in range(t // CUM_BLK):
            rows = slice(blk * CUM_BLK, (blk + 1) * CUM_BLK)
            c = jnp.dot(tri, mask_f32_of_block(rows).astype(BF16), preferred_element_type=F32) + off
            store(rows, c)
            off = c[CUM_BLK - 1:CUM_BLK, :]

    def store_cum(rows, c):
        cum_ref[rows, :] = c

    cumsum_rows(lambda rows: eq[rows].astype(F32), store_cum)
    sel = gt | (eq & (cum_ref[...] <= need))
    self32 = sel.astype(F32)
    cumsum_rows(lambda rows: self32[rows], store_cum)
    cum = cum_ref[...]
    pos_ref[0] = jnp.where(sel, cum.astype(I32) - 1, NEG_POS)
    wsel_ref[0] = jnp.where(sel, aff, 0.0)

    nstart = starts_ref.shape[1]
    tblk = t // (nstart - 1)
    starts_ref[0, 0:1, :] = jnp.zeros((1, V7X_LANES), I32)
    for tb in range(1, nstart):
        starts_ref[0, tb:tb + 1, :] = cum[tb * tblk - 1:tb * tblk, :].astype(I32)

    jj = lax.broadcasted_iota(I32, (1, cap), 1).astype(F32)
    ones = jnp.ones((8, CUM_BLK), BF16)
    for e in range(N_EXPERTS):
        acc = jnp.zeros((8, cap), F32)
        for blk in range(t // CUM_BLK):
            col = cum_ref[blk * CUM_BLK:(blk + 1) * CUM_BLK, e:e + 1]
            acc = acc + jnp.dot(ones, (col <= jj).astype(BF16), preferred_element_type=F32)
        idx_ref[0, e:e + 1, :] = acc[0:1, :].astype(I32)


def _select(aff, bsz, t, n_tok_blk):
    cap = CAPACITY_FACTOR * t // N_EXPERTS
    aff3 = aff.reshape(bsz, t, V7X_LANES)
    tl = pl.BlockSpec((1, t, V7X_LANES), lambda b: (b, 0, 0))
    return pl.pallas_call(
        functools.partial(_select_kernel, cap),
        grid=(bsz,),
        in_specs=[tl],
        out_specs=[pl.BlockSpec((1, N_EXPERTS, cap), lambda b: (b, 0, 0)), tl, tl,
                   pl.BlockSpec((1, n_tok_blk + 1, V7X_LANES), lambda b: (b, 0, 0))],
        out_shape=[jax.ShapeDtypeStruct((bsz, N_EXPERTS, cap), I32),
                   jax.ShapeDtypeStruct((bsz, t, V7X_LANES), I32),
                   jax.ShapeDtypeStruct((bsz, t, V7X_LANES), F32),
                   jax.ShapeDtypeStruct((bsz, n_tok_blk + 1, V7X_LANES), I32)],
        scratch_shapes=[pltpu.VMEM((t, V7X_LANES), F32)],
        compiler_params=_cparams(("parallel",), V7X_VMEM_LIMIT_BYTES),
        name="moe_select",
    )(aff3)


def _ffn_kernel(nf, nd, nbg, cap, cur_ref, nxt_ref, h_hbm, wg_ref, wu_ref, wd_ref, y_ref,
                xin, xb, hid, sem):
    s = pl.program_id(2)
    group = pl.program_id(0) * pl.num_programs(1) + pl.program_id(1)
    ngroups = pl.num_programs(0) * pl.num_programs(1)
    rows = nbg * cap
    per = rows // (nf + nd)
    tf = EXPERT_FF // nf

    def row_copy(idx_ref, j):
        return pltpu.make_async_copy(h_hbm.at[pl.ds(idx_ref[0, 0, j], 1), :],
                                     xin.at[pl.ds(j, 1), :], sem)

    def start_rows(idx_ref, base, n):
        def body(j, c):
            row_copy(idx_ref, base + j).start()
            return c
        lax.fori_loop(0, n, body, 0, unroll=8)

    def wait_rows(idx_ref, n):
        def body(j, c):
            row_copy(idx_ref, j).wait()
            return c
        lax.fori_loop(0, n, body, 0, unroll=8)

    @pl.when(s == 0)
    def _():
        @pl.when(group == 0)
        def _():
            start_rows(cur_ref, 0, rows)
        wait_rows(cur_ref, rows)
        xb[...] = xin[...].astype(BF16)

    @pl.when(group + 1 < ngroups)
    def _():
        start_rows(nxt_ref, s * per, per)

    @pl.when(s < nf)
    def _():
        x = xb[...]
        g = jnp.dot(x, wg_ref[0].astype(BF16), preferred_element_type=F32)
        u = jnp.dot(x, wu_ref[0].astype(BF16), preferred_element_type=F32)
        hid[s] = (g * jax.nn.sigmoid(g) * u).astype(BF16)

    @pl.when(s >= nf)
    def _():
        w = wd_ref[0].astype(BF16)
        acc = None
        for f in range(nf):
            part = jnp.dot(hid[f], w[f * tf:(f + 1) * tf, :], preferred_element_type=F32)
            acc = part if acc is None else acc + part
        for bi in range(nbg):
            y_ref[bi, 0] = acc[bi * cap:(bi + 1) * cap, :].astype(y_ref.dtype)


def _expert_ffn(idx, h, w_gate, w_up, w_down, layer, bsz, t, nf, nd):
    cap = idx.shape[-1]
    d = h.shape[1]
    nbg = 2 if bsz % 2 == 0 else 1
    nh = bsz // nbg
    rows = nbg * cap
    tf, tn = EXPERT_FF // nf, d // nd
    gidx = idx + (jnp.arange(bsz, dtype=I32) * t)[:, None, None]
    gidx = gidx.reshape(nh, nbg, N_EXPERTS, cap).transpose(2, 0, 1, 3).reshape(N_EXPERTS * nh, 1, rows)
    last = N_EXPERTS * nh - 1
    return pl.pallas_call(
        functools.partial(_ffn_kernel, nf, nd, nbg, cap),
        grid=(N_EXPERTS, nh, nf + nd),
        in_specs=[pl.BlockSpec((1, 1, rows), lambda e, hf, s: (e * nh + hf, 0, 0),
                               memory_space=pltpu.SMEM),
                  pl.BlockSpec((1, 1, rows), lambda e, hf, s: (jnp.minimum(e * nh + hf + 1, last), 0, 0),
                               memory_space=pltpu.SMEM),
                  pl.BlockSpec(memory_space=pl.ANY),
                  pl.BlockSpec((None, 1, d, tf), lambda e, hf, s: (layer, e, 0, jnp.minimum(s, nf - 1))),
                  pl.BlockSpec((None, 1, d, tf), lambda e, hf, s: (layer, e, 0, jnp.minimum(s, nf - 1))),
                  pl.BlockSpec((None, 1, EXPERT_FF, tn),
                               lambda e, hf, s: (layer, e, 0, jnp.maximum(s - nf, 0)))],
        out_specs=pl.BlockSpec((nbg, 1, cap, tn), lambda e, hf, s: (hf, e, 0, jnp.maximum(s - nf, 0))),
        out_shape=jax.ShapeDtypeStruct((bsz, N_EXPERTS, cap, d), BF16),
        scratch_shapes=[pltpu.VMEM((rows, d), F32), pltpu.VMEM((rows, d), BF16),
                        pltpu.VMEM((nf, rows, tf), BF16), pltpu.SemaphoreType.DMA],
        compiler_params=_cparams(("arbitrary", "arbitrary", "arbitrary"), V7X_VMEM_LIMIT_BYTES),
        name="moe_ffn",
    )(gidx, gidx, h, w_gate, w_up, w_down)


def _combine_kernel(cap, tblk, starts_ref, x_ref, pos_ref, w_ref, y_hbm, o_ref, kbuf, sems):
    b = pl.program_id(0)
    tb = pl.program_id(1)
    ntb = pl.num_programs(1)
    step = b * ntb + tb
    cur = step % 2
    o_ref[0] = x_ref[0]

    def firsts(bb, tt):
        return [(starts_ref[bb, tt, e] // 16) * 16 for e in range(N_EXPERTS)]

    def offsets(first, p):
        lows = [first[e] + p * ROW_WIN for e in range(N_EXPERTS)]
        return lows, [pl.multiple_of(jnp.minimum(lo, cap - ROW_WIN), 16) for lo in lows]

    def window_copy(bb, e, off, buf):
        return pltpu.make_async_copy(y_hbm.at[bb, e, pl.ds(off, ROW_WIN), :],
                                     kbuf.at[buf, pl.ds(e * ROW_WIN, ROW_WIN), :], sems.at[buf])

    def start_pass(bb, first, p, buf):
        _, offs = offsets(first, p)
        for e in range(N_EXPERTS):
            window_copy(bb, e, offs[e], buf).start()

    def finish_pass(first, p, buf):
        lows, offs = offsets(first, p)
        lane = lax.broadcasted_iota(I32, (tblk, V7X_LANES), 1)
        slot = lane % ROW_WIN
        head = lane < ROW_WIN
        tiles = []
        for g in range(N_EXPERTS * ROW_WIN // V7X_LANES):
            ea, eb = 2 * g, 2 * g + 1
            pe = jnp.where(head, pos_ref[0, :, ea:ea + 1], pos_ref[0, :, eb:eb + 1])
            we = jnp.where(head, w_ref[0, :, ea:ea + 1], w_ref[0, :, eb:eb + 1])
            off = jnp.where(head, offs[ea], offs[eb])
            low = jnp.where(head, lows[ea], lows[eb])
            hit = (pe == off + slot) & (pe >= low)
            tiles.append(jnp.where(hit, we, 0.0).astype(BF16))
        pmat = jnp.concatenate(tiles, axis=1)
        for e in range(N_EXPERTS):
            window_copy(b, e, offs[e], buf).wait()
        o_ref[0] += jnp.dot(pmat, kbuf[buf], preferred_element_type=F32)

    first = firsts(b, tb)
    npass = jnp.int32(1)
    for e in range(N_EXPERTS):
        npass = jnp.maximum(npass, (starts_ref[b, tb + 1, e] - first[e] + ROW_WIN - 1) // ROW_WIN)

    @pl.when(step == 0)
    def _():
        start_pass(b, first, 0, cur)

    @pl.when(step + 1 < pl.num_programs(0) * ntb)
    def _():
        wrap = tb + 1 == ntb
        nb = jnp.where(wrap, b + 1, b)
        nt = jnp.where(wrap, 0, tb + 1)
        start_pass(nb, firsts(nb, nt), 0, 1 - cur)

    finish_pass(first, 0, cur)

    def extra_pass(p, carry):
        start_pass(b, first, p, cur)
        finish_pass(first, p, cur)
        return carry

    lax.fori_loop(1, npass, extra_pass, 0)


def _combine(x, pos, wsel, starts, y, tblk):
    bsz, t, d = x.shape
    cap = y.shape[2]
    return pl.pallas_call(
        functools.partial(_combine_kernel, cap, tblk),
        grid=(bsz, t // tblk),
        in_specs=[pl.BlockSpec(memory_space=pltpu.SMEM),
                  pl.BlockSpec((1, tblk, d), lambda b, i: (b, i, 0)),
                  pl.BlockSpec((1, tblk, V7X_LANES), lambda b, i: (b, i, 0)),
                  pl.BlockSpec((1, tblk, V7X_LANES), lambda b, i: (b, i, 0)),
                  pl.BlockSpec(memory_space=pl.ANY)],
        out_specs=pl.BlockSpec((1, tblk, d), lambda b, i: (b, i, 0)),
        out_shape=jax.ShapeDtypeStruct((bsz, t, d), F32),
        scratch_shapes=[pltpu.VMEM((2, N_EXPERTS * ROW_WIN, d), BF16), pltpu.SemaphoreType.DMA((2,))],
        compiler_params=_cparams(("arbitrary", "arbitrary")),
        name="moe_combine",
    )(starts, x, pos, wsel, y)


def _tiles(n, t):
    pick = lambda want, total: want if total % want == 0 else total
    return dict(
        tm_proj=pick(1024, n), tn_proj=768,
        tq=pick(512, t), tk=pick(1024, t), lb=pick(512, t),
        tm_gsum=pick(1024, n), tn_gsum=256, tm_oproj=pick(512, n),
        tm_router=pick(512, n),
        nf_ffn=4, nd_ffn=4,
        tblk=pick(256, t),
    )


def kernel(x, positions, norm_mix, w_in, conv_a, q_norm, k_norm, lambda_q1, lambda_k1, lambda_q2, lambda_k2, subln, conv_c, a_log_f, a_log_b, dt_bias_f, dt_bias_b, o_norm, w_out_a, w_out_b, w_out_c, w_gate, b_gate, w_o, norm_ffn, w_router, w_e_gate, w_e_up, w_e_down):
    bsz, t, d = x.shape
    n = bsz * t
    depth = w_in.shape[0]
    ts = _tiles(n, t)
    tables = _rope_tables(positions)
    pad_small = jnp.zeros((d, V7X_LANES - N_SMALL), F32)
    tile2 = lambda g: jnp.concatenate([g, g]).reshape(1, V7X_LANES)
    pad64 = lambda vv: jnp.concatenate([vv, jnp.zeros_like(vv)])

    xf = x.reshape(n, d)
    for l in range(depth):
        lam_init = 0.8 - 0.6 * math.exp(-0.3 * l)
        w_small = jnp.concatenate([w_in[l, :, N_MAIN:], pad_small], axis=1)[None]
        main = _norm_matmul(xf, norm_mix[l], w_in, l, N_MAIN, BF16, ts["tm_proj"], ts["tn_proj"])
        small = _norm_matmul(xf, norm_mix[l], w_small, 0, V7X_LANES, F32, ts["tm_proj"], V7X_LANES)
        main3 = main.reshape(bsz, t, N_MAIN)
        small3 = small.reshape(bsz, t, V7X_LANES)

        mix_a = _mixer_a(main3, conv_a[l])
        lamv = jnp.stack([pad64(lambda_q1[l]), pad64(lambda_k1[l]),
                          pad64(lambda_q2[l]), pad64(lambda_k2[l])])
        mix_b = _diff_attention(main3, tables, tile2(q_norm[l]), tile2(k_norm[l]), lamv,
                                subln[l].reshape(1, V7X_LANES), lam_init, ts["tq"], ts["tk"])
        qkv_c = _gdn_prep(main3, conv_c[l])
        o_f = _gdn_direction(qkv_c, small3, a_log_f[l], dt_bias_f[l], False, ts["lb"])
        o_b = _gdn_direction(qkv_c, small3, a_log_b[l], dt_bias_b[l], True, ts["lb"])

        w_out = jnp.concatenate([w_out_a[l], w_out_b[l], w_out_c[l]], axis=0).astype(BF16)
        gsum = _gated_sum(xf, norm_mix[l], mix_a.reshape(n, W_A), mix_b.reshape(n, W_B),
                          o_f.reshape(n, W_C), o_b.reshape(n, W_C), main, o_norm[l],
                          w_out, w_gate[l].astype(BF16), b_gate[l].reshape(1, N_BRANCH * d),
                          ts["tm_gsum"], ts["tn_gsum"])
        x1 = _out_proj(xf, gsum, w_o[l].astype(BF16), ts["tm_oproj"])

        h, aff = _router(x1, norm_ffn[l], w_router[l], ts["tm_router"])
        idx, pos, wsel, starts = _select(aff, bsz, t, t // ts["tblk"])
        y = _expert_ffn(idx, h, w_e_gate, w_e_up, w_e_down, l, bsz, t, ts["nf_ffn"], ts["nd_ffn"])
        xf = _combine(x1.reshape(bsz, t, d), pos, wsel, starts, y, ts["tblk"]).reshape(n, d)
    return xf.reshape(bsz, t, d)
```

```python
import functools
import math

import jax
import jax.numpy as jnp
from jax import lax
from jax.experimental import pallas as pl
from jax.experimental.pallas import tpu as pltpu

F32 = jnp.float32
BF16 = jnp.bfloat16
I32 = jnp.int32

D_MODEL = 2048
HEAD_DIM = 128
W_A = D_MODEL // 4
DH_B = 64
DV_B = 2 * DH_B
H_B = (3 * D_MODEL // 8) // DV_B
W_B = H_B * DV_B
ROT_DIM = DH_B // 4
ROPE_THETA = 500000.0
DK_C = HEAD_DIM
DV_C = HEAD_DIM
H_C = (3 * D_MODEL // 8) // DV_C
W_C = H_C * DV_C
CHUNK = 64
N_BRANCH = 3
N_EXPERTS = 16
CAPACITY_FACTOR = 2
EXPERT_FF = D_MODEL // 2
EPS = 1e-6

N_MAIN = 3 * W_A + 3 * W_B + 4 * W_C
N_SMALL = 4 * H_C
COL_A = 0
COL_B = 3 * W_A
COL_C = COL_B + 3 * W_B

V7X_LANES = 128
BF16_SUBLANES = 16
V7X_VMEM_LIMIT_BYTES = 56 * 1024 * 1024
ROW_WIN = 64
NEG_POS = -(1 << 20)


def _cparams(sem, vmem=None):
    return pltpu.CompilerParams(dimension_semantics=sem, vmem_limit_bytes=vmem)


def _split_dot(a, b):
    hi = a.astype(BF16)
    lo = (a - hi.astype(F32)).astype(BF16)
    return (jnp.dot(hi, b, preferred_element_type=F32)
            + jnp.dot(lo, b, preferred_element_type=F32))


def _norm_matmul_kernel(x_ref, g_ref, w_ref, ws_ref, o_ref, os_ref, xn_ref):
    @pl.when(pl.program_id(1) == 0)
    def _():
        x = x_ref[...]
        xn = x * lax.rsqrt(jnp.mean(x * x, axis=-1, keepdims=True) + EPS) * g_ref[...]
        xn_ref[...] = xn.astype(BF16)
        os_ref[...] = jnp.dot(xn_ref[...], ws_ref[...].astype(BF16), preferred_element_type=F32)

    o_ref[...] = jnp.dot(xn_ref[...], w_ref[...].astype(BF16),
                         preferred_element_type=F32).astype(o_ref.dtype)


def _norm_matmul(x, gain, w, layer, nout, w_small, tm, tn):
    n, d = x.shape
    return pl.pallas_call(
        _norm_matmul_kernel,
        grid=(n // tm, nout // tn),
        in_specs=[pl.BlockSpec((tm, d), lambda i, j: (i, 0)),
                  pl.BlockSpec((1, d), lambda i, j: (0, 0)),
                  pl.BlockSpec((None, d, tn), lambda i, j: (layer, 0, j)),
                  pl.BlockSpec((d, V7X_LANES), lambda i, j: (0, 0))],
        out_specs=[pl.BlockSpec((tm, tn), lambda i, j: (i, j)),
                   pl.BlockSpec((tm, V7X_LANES), lambda i, j: (i, 0))],
        out_shape=[jax.ShapeDtypeStruct((n, nout), BF16),
                   jax.ShapeDtypeStruct((n, V7X_LANES), F32)],
        scratch_shapes=[pltpu.VMEM((tm, d), BF16)],
        compiler_params=_cparams(("parallel", "arbitrary"), V7X_VMEM_LIMIT_BYTES),
        name="norm_matmul",
    )(x, gain.reshape(1, d), w, w_small)


def _conv3(x, w):
    t = x.shape[0]
    row = lax.broadcasted_iota(I32, x.shape, 0)
    prev = jnp.where(row == 0, 0.0, pltpu.roll(x, 1, 0))
    nxt = jnp.where(row == t - 1, 0.0, pltpu.roll(x, t - 1, 0))
    return prev * w[0:1, :] + x * w[1:2, :] + nxt * w[2:3, :]


def _mixer_a_kernel(b_ref, c_ref, v_ref, w_ref, o_ref):
    cv = c_ref[0].astype(F32) * v_ref[0].astype(F32)
    o_ref[0] = (b_ref[0].astype(F32) * _conv3(cv, w_ref[...])).astype(o_ref.dtype)


def _mixer_a(main, conv_w):
    bsz, t, _ = main.shape
    nblk = W_A // V7X_LANES
    c0 = COL_A // V7X_LANES
    spec = lambda off: pl.BlockSpec((1, t, V7X_LANES), lambda b, c: (b, 0, c0 + off + c))
    return pl.pallas_call(
        _mixer_a_kernel,
        grid=(bsz, nblk),
        in_specs=[spec(0), spec(nblk), spec(2 * nblk),
                  pl.BlockSpec((3, V7X_LANES), lambda b, c: (0, c))],
        out_specs=pl.BlockSpec((1, t, V7X_LANES), lambda b, c: (b, 0, c)),
        out_shape=jax.ShapeDtypeStruct((bsz, t, W_A), BF16),
        compiler_params=_cparams(("parallel", "parallel")),
        name="mixer_a",
    )(main, main, main, conv_w)


def _gdn_prep_kernel(x_ref, w_ref, o_ref):
    c = pl.program_id(1)
    y = _conv3(x_ref[0].astype(F32), w_ref[...])
    y = y * jax.nn.sigmoid(y)
    inv = lax.rsqrt(jnp.sum(y * y, axis=-1, keepdims=True) + EPS)
    scale = jnp.where(c < H_C, inv * (DK_C ** -0.5), jnp.where(c < 2 * H_C, inv, 1.0))
    o_ref[0] = (y * scale).astype(o_ref.dtype)


def _gdn_prep(main, conv_w):
    bsz, t, _ = main.shape
    nblk = 3 * W_C // V7X_LANES
    c0 = COL_C // V7X_LANES
    return pl.pallas_call(
        _gdn_prep_kernel,
        grid=(bsz, nblk),
        in_specs=[pl.BlockSpec((1, t, V7X_LANES), lambda b, c: (b, 0, c0 + c)),
                  pl.BlockSpec((3, V7X_LANES), lambda b, c: (0, c))],
        out_specs=pl.BlockSpec((1, t, V7X_LANES), lambda b, c: (b, 0, c)),
        out_shape=jax.ShapeDtypeStruct((bsz, t, 3 * W_C), BF16),
        compiler_params=_cparams(("parallel", "parallel")),
        name="gdn_prep",
    )(main, conv_w)


def _rope_kernel(pos_ref, c_ref, s1_ref, s2_ref):
    pos = pos_ref[0].astype(F32)
    lane = lax.broadcasted_iota(I32, (1, V7X_LANES), 1)
    l64 = lane % DH_B
    half = ROT_DIM // 2
    fidx = jnp.where(l64 < half, l64, l64 - half).astype(F32)
    inv_freq = jnp.exp(fidx * (-math.log(ROPE_THETA) * 2.0 / ROT_DIM))
    ang = pos * inv_freq
    cos, sin = jnp.cos(ang), jnp.sin(ang)
    first = l64 < half
    second = (l64 >= half) & (l64 < ROT_DIM)
    c_ref[0] = jnp.where(first | second, cos, 1.0)
    s1_ref[0] = jnp.where(first, -sin, 0.0)
    s2_ref[0] = jnp.where(second, sin, 0.0)


def _rope_tables(positions):
    bsz, t = positions.shape
    shp = jax.ShapeDtypeStruct((bsz, t, V7X_LANES), F32)
    spec = pl.BlockSpec((1, t, V7X_LANES), lambda b: (b, 0, 0))
    return pl.pallas_call(
        _rope_kernel,
        grid=(bsz,),
        in_specs=[pl.BlockSpec((1, t, 1), lambda b: (b, 0, 0))],
        out_specs=[spec, spec, spec],
        out_shape=[shp, shp, shp],
        compiler_params=_cparams(("parallel",)),
        name="rope_tables",
    )(positions.reshape(bsz, t, 1))


def _qk_prep(x, gain, cos, s1, s2):
    r = lax.broadcasted_iota(I32, (V7X_LANES, V7X_LANES), 0) // DH_B
    c = lax.broadcasted_iota(I32, (V7X_LANES, V7X_LANES), 1) // DH_B
    blockdiag = (r == c).astype(BF16)
    ssq = _split_dot(x * x, blockdiag)
    xn = x * lax.rsqrt(ssq * (1.0 / DH_B) + EPS) * gain
    half = ROT_DIM // 2
    return xn * cos + pltpu.roll(xn, V7X_LANES - half, 1) * s1 + pltpu.roll(xn, half, 1) * s2


def _attn_kernel(lam_init, tq, tk, q_ref, k_ref, v_ref, c_ref, s1_ref, s2_ref,
                 qg_ref, kg_ref, lamv_ref, sub_ref, o_ref, kp_ref, vx_ref, acc_ref):
    qi = pl.program_id(2)
    t = kp_ref.shape[0]

    @pl.when(qi == 0)
    def _():
        kp_ref[...] = _qk_prep(k_ref[0].astype(F32), kg_ref[...],
                               c_ref[0], s1_ref[0], s2_ref[0]).astype(BF16)
        vx_ref[:DV_B, :] = v_ref[0].astype(F32).T.astype(BF16)
        vx_ref[DV_B:, :] = jnp.ones((vx_ref.shape[0] - DV_B, t), BF16)

    rows = pl.ds(pl.multiple_of(qi * tq, tq), tq)
    q = _qk_prep(q_ref[0].astype(F32), qg_ref[...], c_ref[0, rows, :], s1_ref[0, rows, :],
                 s2_ref[0, rows, :]) * (DH_B ** -0.5 * math.log2(math.e))
    lane = lax.broadcasted_iota(I32, (tq, V7X_LANES), 1)
    qt = [jnp.where(keep, q, 0.0).T.astype(BF16) for keep in (lane < DH_B, lane >= DH_B)]

    def scores(j):
        kj = kp_ref[j * tk:(j + 1) * tk, :]
        return [jnp.dot(kj, qc, preferred_element_type=F32) for qc in qt]

    acc_ref[...] = jnp.zeros_like(acc_ref)
    m = [jnp.full((1, tq), -jnp.inf, F32)] * 2
    nkb = t // tk
    ahead = 2
    pending = [scores(j) for j in range(min(ahead, nkb))]
    for j in range(nkb):
        cur = pending.pop(0)
        if j + ahead < nkb:
            pending.append(scores(j + ahead))
        vj = vx_ref[:, j * tk:(j + 1) * tk]
        for c in range(2):
            m_new = jnp.maximum(m[c], jnp.max(cur[c], axis=0, keepdims=True))
            e = jnp.exp2(cur[c] - m_new).astype(BF16)
            acc_ref[c] = acc_ref[c] * jnp.exp2(m[c] - m_new) + jnp.dot(vj, e, preferred_element_type=F32)
            m[c] = m_new
    pv = [acc_ref[c, :DV_B, :] for c in range(2)]
    l = [acc_ref[c, DV_B:DV_B + 1, :] for c in range(2)]
    lv = lamv_ref[...]
    lam = (jnp.exp(jnp.sum(lv[0:1] * lv[1:2], axis=-1, keepdims=True))
           - jnp.exp(jnp.sum(lv[2:3] * lv[3:4], axis=-1, keepdims=True)) + lam_init)
    o = (pv[0] / l[0] - lam * (pv[1] / l[1])).T
    o = o * lax.rsqrt(jnp.mean(o * o, axis=-1, keepdims=True) + EPS) * sub_ref[...]
    o_ref[0] = (o * (1.0 - lam_init)).astype(o_ref.dtype)


def _diff_attention(main, tables, q_gain, k_gain, lamv, subln, lam_init, tq, tk):
    bsz, t, _ = main.shape
    cq = COL_B // V7X_LANES
    ck = cq + H_B
    cv = ck + H_B
    tab = pl.BlockSpec((1, t, V7X_LANES), lambda b, h, i: (b, 0, 0))
    vec = pl.BlockSpec((1, V7X_LANES), lambda b, h, i: (0, 0))
    return pl.pallas_call(
        functools.partial(_attn_kernel, lam_init, tq, tk),
        grid=(bsz, H_B, t // tq),
        in_specs=[pl.BlockSpec((1, tq, V7X_LANES), lambda b, h, i: (b, i, cq + h)),
                  pl.BlockSpec((1, t, V7X_LANES), lambda b, h, i: (b, 0, ck + h)),
                  pl.BlockSpec((1, t, V7X_LANES), lambda b, h, i: (b, 0, cv + h)),
                  tab, tab, tab, vec, vec,
                  pl.BlockSpec((4, V7X_LANES), lambda b, h, i: (0, 0)),
                  vec],
        out_specs=pl.BlockSpec((1, tq, V7X_LANES), lambda b, h, i: (b, i, h)),
        out_shape=jax.ShapeDtypeStruct((bsz, t, W_B), BF16),
        scratch_shapes=[pltpu.VMEM((t, V7X_LANES), BF16),
                        pltpu.VMEM((DV_B + BF16_SUBLANES, t), BF16),
                        pltpu.VMEM((2, DV_B + BF16_SUBLANES, tq), F32)],
        compiler_params=_cparams(("parallel", "parallel", "arbitrary"), V7X_VMEM_LIMIT_BYTES),
        name="diff_attention",
    )(main, main, main, *tables, q_gain, k_gain, lamv, subln)


def _gdn_kernel(rev, lb, q_ref, k_ref, v_ref, sm_ref, alog_ref, dtb_ref, o_ref, s_ref):
    @pl.when(pl.program_id(1) == 0)
    def _():
        s_ref[...] = jnp.zeros_like(s_ref)

    off_b = H_C if rev else 0
    off_a = 3 * H_C if rev else 2 * H_C
    ri = lax.broadcasted_iota(I32, (CHUNK, CHUNK), 0)
    ci = lax.broadcasted_iota(I32, (CHUNK, CHUNK), 1)
    incl = (ri <= ci) if rev else (ri >= ci)
    strict = (ri < ci) if rev else (ri > ci)
    cum_mat = incl.astype(BF16)
    eye = (ri == ci).astype(F32)
    last = 0 if rev else CHUNK - 1

    nchunk = lb // CHUNK
    order = list(range(nchunk - 1, -1, -1) if rev else range(nchunk))
    units = [(c, h) for c in order for h in range(H_C)]
    nt = (((1,), (1,)), ((), ()))
    dot = functools.partial(jnp.dot, preferred_element_type=F32)

    gates = {}
    for c in order:
        rows = slice(c * CHUNK, (c + 1) * CHUNK)
        sm = sm_ref[0, rows, :]
        g_all = -jnp.exp(alog_ref[...]) * jax.nn.softplus(sm + dtb_ref[...])
        gc_all = _split_dot_left(cum_mat, g_all)
        gates[c] = (jax.nn.sigmoid(sm), gc_all, gc_all.T)

    pre = {}
    for c, h in units:
        rows = slice(c * CHUNK, (c + 1) * CHUNK)
        lanes = slice(h * HEAD_DIM, (h + 1) * HEAD_DIM)
        beta_all, gc_all, gct_all = gates[c]
        q = q_ref[0, rows, lanes].astype(F32)
        k = k_ref[0, rows, lanes]
        kf = k.astype(F32)
        beta = beta_all[:, off_b + h:off_b + h + 1]
        gcol = gc_all[:, off_a + h:off_a + h + 1]
        grow = gct_all[off_a + h:off_a + h + 1, :]
        glast = gcol[last:last + 1, :]
        decay = jnp.where(incl, jnp.exp(jnp.where(incl, gcol - grow, 0.0)), 0.0)
        kb = kf * beta
        eg = jnp.exp(gcol)
        rhs = jnp.concatenate([v_ref[0, rows, lanes].astype(F32) * beta, kb * eg], axis=-1).astype(BF16)
        pre[c, h] = dict(k=k, kb=kb, decay=decay, rhs=rhs, glast=glast,
                         q=q, q_dec=q * eg, k_dec=(kf * jnp.exp(glast - gcol)).astype(BF16))

    for u in units:
        d = pre[u]
        res = lax.dot_general(jnp.concatenate([d["kb"], d["q"]], axis=0).astype(BF16), d["k"], nt,
                              preferred_element_type=F32)
        d["low"] = jnp.where(strict, res[:CHUNK] * d["decay"], 0.0)
        d["attn"] = (res[CHUNK:] * d["decay"]).astype(BF16)
    for u in units:
        d = pre[u]
        lb16 = d["low"].astype(BF16)
        d["n"] = dot(lb16, lb16)
        d["p"] = eye - d["low"]
    for r in range(1, 6):
        for u in units:
            d = pre[u]
            nb16 = d["n"].astype(BF16)
            if r < 5:
                res = dot(jnp.concatenate([d["n"], d["p"]], axis=0).astype(BF16), nb16)
                d["n"] = res[:CHUNK]
                d["p"] = d["p"] + res[CHUNK:]
            else:
                d["p"] = d["p"] + dot(d["p"].astype(BF16), nb16)
    for u in units:
        d = pre[u]
        d["sol"] = dot(d["p"].astype(BF16), d["rhs"])

    st = {h: s_ref[h] for h in range(H_C)}
    for c in order:
        rows = slice(c * CHUNK, (c + 1) * CHUNK)
        res, vb = {}, {}
        for h in range(H_C):
            d = pre[c, h]
            lhs = jnp.concatenate([d["sol"][:, DV_C:], d["q_dec"]], axis=0).astype(BF16)
            res[h] = dot(lhs, st[h].astype(BF16))
        for h in range(H_C):
            vb[h] = (pre[c, h]["sol"][:, :DV_C] - res[h][:CHUNK]).astype(BF16)
        for h in range(H_C):
            d = pre[c, h]
            lanes = slice(h * HEAD_DIM, (h + 1) * HEAD_DIM)
            o_ref[0, rows, lanes] = (res[h][CHUNK:] + dot(d["attn"], vb[h])).astype(o_ref.dtype)
            st[h] = st[h] * jnp.exp(d["glast"]) + lax.dot_general(
                d["k_dec"], vb[h], (((0,), (0,)), ((), ())), preferred_element_type=F32)
    for h in range(H_C):
        s_ref[h] = st[h]


def _split_dot_left(a, b):
    hi = b.astype(BF16)
    lo = (b - hi.astype(F32)).astype(BF16)
    return (jnp.dot(a, hi, preferred_element_type=F32)
            + jnp.dot(a, lo, preferred_element_type=F32))


def _gdn_direction(qkv, small, a_log, dt_bias, rev, lb):
    bsz, t, _ = qkv.shape
    nb = t // lb
    blk = (lambda i: nb - 1 - i) if rev else (lambda i: i)
    off_a = 3 * H_C if rev else 2 * H_C
    pad = lambda vec: jnp.zeros((1, V7X_LANES), F32).at[0, off_a:off_a + H_C].set(vec)
    col = lambda j: pl.BlockSpec((1, lb, W_C), lambda b, i: (b, blk(i), j))
    vec = pl.BlockSpec((1, V7X_LANES), lambda b, i: (0, 0))
    return pl.pallas_call(
        functools.partial(_gdn_kernel, rev, lb),
        grid=(bsz, nb),
        in_specs=[col(0), col(1), col(2),
                  pl.BlockSpec((1, lb, V7X_LANES), lambda b, i: (b, blk(i), 0)),
                  vec, vec],
        out_specs=pl.BlockSpec((1, lb, W_C), lambda b, i: (b, blk(i), 0)),
        out_shape=jax.ShapeDtypeStruct((bsz, t, W_C), BF16),
        scratch_shapes=[pltpu.VMEM((H_C, DK_C, DV_C), F32)],
        compiler_params=_cparams(("parallel", "arbitrary")),
        name="gdn_bwd" if rev else "gdn_fwd",
    )(qkv, qkv, qkv, small, pad(a_log), pad(dt_bias))


def _gated_sum_kernel(x_ref, gn_ref, ma_ref, mb_ref, of_ref, ob_ref, gc_ref, og_ref,
                      wout_ref, wg0_ref, wg1_ref, wg2_ref, bg0_ref, bg1_ref, bg2_ref,
                      o_ref, xn_ref, mix_ref):
    j = pl.program_id(1)

    @pl.when(j == 0)
    def _():
        x = x_ref[...]
        xn = x * lax.rsqrt(jnp.mean(x * x, axis=-1, keepdims=True) + EPS) * gn_ref[...]
        xn_ref[...] = xn.astype(BF16)
        mix_ref[:, 0:W_A] = ma_ref[...]
        mix_ref[:, W_A:W_A + W_B] = mb_ref[...]
        for h in range(H_C):
            lanes = slice(h * HEAD_DIM, (h + 1) * HEAD_DIM)
            o = of_ref[:, lanes].astype(F32) + ob_ref[:, lanes].astype(F32)
            o = o * lax.rsqrt(jnp.mean(o * o, axis=-1, keepdims=True) + EPS) * og_ref[...]
            g = gc_ref[:, lanes].astype(F32)
            c0 = W_A + W_B + h * HEAD_DIM
            mix_ref[:, c0:c0 + HEAD_DIM] = (o * (g * jax.nn.sigmoid(g))).astype(BF16)

    xn = xn_ref[...]
    s = None
    for lo, hi, wg_ref, bg_ref in ((0, W_A, wg0_ref, bg0_ref),
                                   (W_A, W_A + W_B, wg1_ref, bg1_ref),
                                   (W_A + W_B, D_MODEL, wg2_ref, bg2_ref)):
        y = jnp.dot(mix_ref[:, lo:hi], wout_ref[lo:hi, :], preferred_element_type=F32)
        gate = jax.nn.sigmoid(jnp.dot(xn, wg_ref[...], preferred_element_type=F32) + bg_ref[...])
        s = gate * y if s is None else s + gate * y
    o_ref[...] = s.astype(o_ref.dtype)


def _gated_sum(x, norm_gain, mix_a, mix_b, o_f, o_b, main, o_gain, w_out, w_gate, b_gate, tm, tn):
    n, d = x.shape
    nj = d // tn
    cg = (COL_C + 3 * W_C) // W_C
    row = lambda w: pl.BlockSpec((tm, w), lambda i, j: (i, 0))
    wg = lambda br: pl.BlockSpec((d, tn), lambda i, j: (0, br * nj + j))
    bg = lambda br: pl.BlockSpec((1, tn), lambda i, j: (0, br * nj + j))
    return pl.pallas_call(
        _gated_sum_kernel,
        grid=(n // tm, nj),
        in_specs=[row(d),
                  pl.BlockSpec((1, d), lambda i, j: (0, 0)),
                  row(W_A), row(W_B), row(W_C), row(W_C),
                  pl.BlockSpec((tm, W_C), lambda i, j: (i, cg)),
                  pl.BlockSpec((1, HEAD_DIM), lambda i, j: (0, 0)),
                  pl.BlockSpec((d, tn), lambda i, j: (0, j)),
                  wg(0), wg(1), wg(2), bg(0), bg(1), bg(2)],
        out_specs=pl.BlockSpec((tm, tn), lambda i, j: (i, j)),
        out_shape=jax.ShapeDtypeStruct((n, d), BF16),
        scratch_shapes=[pltpu.VMEM((tm, d), BF16), pltpu.VMEM((tm, d), BF16)],
        compiler_params=_cparams(("parallel", "arbitrary"), V7X_VMEM_LIMIT_BYTES),
        name="gated_sum",
    )(x, norm_gain.reshape(1, d), mix_a, mix_b, o_f, o_b, main, o_gain.reshape(1, HEAD_DIM),
      w_out, w_gate, w_gate, w_gate, b_gate, b_gate, b_gate)


def _out_proj_kernel(x_ref, s_ref, w_ref, o_ref):
    o_ref[...] = x_ref[...] + jnp.dot(s_ref[...], w_ref[...], preferred_element_type=F32)


def _out_proj(x, s, w_o, tm):
    n, d = x.shape
    row = pl.BlockSpec((tm, d), lambda i: (i, 0))
    return pl.pallas_call(
        _out_proj_kernel,
        grid=(n // tm,),
        in_specs=[row, row, pl.BlockSpec((d, d), lambda i: (0, 0))],
        out_specs=row,
        out_shape=jax.ShapeDtypeStruct((n, d), F32),
        compiler_params=_cparams(("parallel",), V7X_VMEM_LIMIT_BYTES),
        name="out_proj",
    )(x, s, w_o)


def _router_kernel(x_ref, g_ref, wh_ref, wl_ref, h_ref, aff_ref):
    x = x_ref[...]
    h = x * lax.rsqrt(jnp.mean(x * x, axis=-1, keepdims=True) + EPS) * g_ref[...]
    h_ref[...] = h
    hh = h.astype(BF16)
    hl = (h - hh.astype(F32)).astype(BF16)
    logits = (jnp.dot(hh, wh_ref[...], preferred_element_type=F32)
              + jnp.dot(hl, wh_ref[...], preferred_element_type=F32)
              + jnp.dot(hh, wl_ref[...], preferred_element_type=F32))
    lane = lax.broadcasted_iota(I32, logits.shape, 1)
    valid = lane < N_EXPERTS
    logits = jnp.where(valid, logits, -jnp.inf)
    m = jnp.max(logits, axis=-1, keepdims=True)
    e = jnp.where(valid, jnp.exp(logits - m), 0.0)
    aff_ref[...] = e / jnp.sum(e, axis=-1, keepdims=True)


def _router(x, gain, w_router, tm):
    n, d = x.shape
    wpad = jnp.zeros((d, V7X_LANES), F32).at[:, :N_EXPERTS].set(w_router)
    w_hi = wpad.astype(BF16)
    w_lo = (wpad - w_hi.astype(F32)).astype(BF16)
    wspec = pl.BlockSpec((d, V7X_LANES), lambda i: (0, 0))
    return pl.pallas_call(
        _router_kernel,
        grid=(n // tm,),
        in_specs=[pl.BlockSpec((tm, d), lambda i: (i, 0)),
                  pl.BlockSpec((1, d), lambda i: (0, 0)), wspec, wspec],
        out_specs=[pl.BlockSpec((tm, d), lambda i: (i, 0)),
                   pl.BlockSpec((tm, V7X_LANES), lambda i: (i, 0))],
        out_shape=[jax.ShapeDtypeStruct((n, d), F32),
                   jax.ShapeDtypeStruct((n, V7X_LANES), F32)],
        compiler_params=_cparams(("parallel",), V7X_VMEM_LIMIT_BYTES),
        name="router",
    )(x, gain.reshape(1, d), w_hi, w_lo)


CUM_BLK = 256


def _select_kernel(cap, aff_ref, idx_ref, pos_ref, wsel_ref, starts_ref, cum_ref):
    t = aff_ref.shape[1]
    aff = aff_ref[0]

    def search(i, lo):
        cand = lo | jnp.left_shift(jnp.int32(1), 30 - i)
        cnt = jnp.sum((aff >= pltpu.bitcast(cand, F32)).astype(I32), axis=0, keepdims=True)
        return jnp.where(cnt >= cap, cand, lo)

    thr_bits = lax.fori_loop(0, 31, search, jnp.zeros((1, V7X_LANES), I32))
    thr = pltpu.bitcast(thr_bits, F32)
    gt = aff > thr
    eq = aff == thr
    need = (cap - jnp.sum(gt.astype(I32), axis=0, keepdims=True)).astype(F32)

    ri = lax.broadcasted_iota(I32, (CUM_BLK, CUM_BLK), 0)
    ci = lax.broadcasted_iota(I32, (CUM_BLK, CUM_BLK), 1)
    tri = (ri >= ci).astype(BF16)

    def cumsum_rows(mask_f32_of_block, store):
        off = jnp.zeros((1, V7X_LANES), F32)
        for blk in range(t // CUM_BLK):
            rows = slice(blk * CUM_BLK, (blk + 1) * CUM_BLK)
            c = jnp.dot(tri, mask_f32_of_block(rows).astype(BF16), preferred_element_type=F32) + off
            store(rows, c)
            off = c[CUM_BLK - 1:CUM_BLK, :]

    def store_cum(rows, c):
        cum_ref[rows, :] = c

    cumsum_rows(lambda rows: eq[rows].astype(F32), store_cum)
    sel = gt | (eq & (cum_ref[...] <= need))
    self32 = sel.astype(F32)
    cumsum_rows(lambda rows: self32[rows], store_cum)
    cum = cum_ref[...]
    pos_ref[0] = jnp.where(sel, cum.astype(I32) - 1, NEG_POS)
    wsel_ref[0] = jnp.where(sel, aff, 0.0)

    nstart = starts_ref.shape[1]
    tblk = t // (nstart - 1)
    starts_ref[0, 0:1, :] = jnp.zeros((1, V7X_LANES), I32)
    for tb in range(1, nstart):
        starts_ref[0, tb:tb + 1, :] = cum[tb * tblk - 1:tb * tblk, :].astype(I32)

    jj = lax.broadcasted_iota(I32, (1, cap), 1).astype(F32)
    ones = jnp.ones((8, CUM_BLK), BF16)
    for e in range(N_EXPERTS):
        acc = jnp.zeros((8, cap), F32)
        for blk in range(t // CUM_BLK):
            col = cum_ref[blk * CUM_BLK:(blk + 1) * CUM_BLK, e:e + 1]
            acc = acc + jnp.dot(ones, (col <= jj).astype(BF16), preferred_element_type=F32)
        idx_ref[0, e:e + 1, :] = acc[0:1, :].astype(I32)


def _select(aff, bsz, t, n_tok_blk):
    cap = CAPACITY_FACTOR * t // N_EXPERTS
    aff3 = aff.reshape(bsz, t, V7X_LANES)
    tl = pl.BlockSpec((1, t, V7X_LANES), lambda b: (b, 0, 0))
    return pl.pallas_call(
        functools.partial(_select_kernel, cap),
        grid=(bsz,),
        in_specs=[tl],
        out_specs=[pl.BlockSpec((1, N_EXPERTS, cap), lambda b: (b, 0, 0)), tl, tl,
                   pl.BlockSpec((1, n_tok_blk + 1, V7X_LANES), lambda b: (b, 0, 0))],
        out_shape=[jax.ShapeDtypeStruct((bsz, N_EXPERTS, cap), I32),
                   jax.ShapeDtypeStruct((bsz, t, V7X_LANES), I32),
                   jax.ShapeDtypeStruct((bsz, t, V7X_LANES), F32),
                   jax.ShapeDtypeStruct((bsz, n_tok_blk + 1, V7X_LANES), I32)],
        scratch_shapes=[pltpu.VMEM((t, V7X_LANES), F32)],
        compiler_params=_cparams(("parallel",), V7X_VMEM_LIMIT_BYTES),
        name="moe_select",
    )(aff3)


def _ffn_kernel(nf, nd, nbg, cap, cur_ref, nxt_ref, h_hbm, wg_ref, wu_ref, wd_ref, y_ref,
                xin, xb, hid, sem):
    s = pl.program_id(2)
    group = pl.program_id(0) * pl.num_programs(1) + pl.program_id(1)
    ngroups = pl.num_programs(0) * pl.num_programs(1)
    rows = nbg * cap
    per = rows // (nf + nd)
    tf = EXPERT_FF // nf

    def row_copy(idx_ref, j):
        return pltpu.make_async_copy(h_hbm.at[pl.ds(idx_ref[0, 0, j], 1), :],
                                     xin.at[pl.ds(j, 1), :], sem)

    def start_rows(idx_ref, base, n):
        def body(j, c):
            row_copy(idx_ref, base + j).start()
            return c
        lax.fori_loop(0, n, body, 0, unroll=8)

    def wait_rows(idx_ref, n):
        def body(j, c):
            row_copy(idx_ref, j).wait()
            return c
        lax.fori_loop(0, n, body, 0, unroll=8)

    @pl.when(s == 0)
    def _():
        @pl.when(group == 0)
        def _():
            start_rows(cur_ref, 0, rows)
        wait_rows(cur_ref, rows)
        xb[...] = xin[...].astype(BF16)

    @pl.when(group + 1 < ngroups)
    def _():
        start_rows(nxt_ref, s * per, per)

    @pl.when(s < nf)
    def _():
        x = xb[...]
        g = jnp.dot(x, wg_ref[0].astype(BF16), preferred_element_type=F32)
        u = jnp.dot(x, wu_ref[0].astype(BF16), preferred_element_type=F32)
        hid[s] = (g * jax.nn.sigmoid(g) * u).astype(BF16)

    @pl.when(s >= nf)
    def _():
        w = wd_ref[0].astype(BF16)
        acc = None
        for f in range(nf):
            part = jnp.dot(hid[f], w[f * tf:(f + 1) * tf, :], preferred_element_type=F32)
            acc = part if acc is None else acc + part
        for bi in range(nbg):
            y_ref[bi, 0] = acc[bi * cap:(bi + 1) * cap, :].astype(y_ref.dtype)


def _expert_ffn(idx, h, w_gate, w_up, w_down, layer, bsz, t, nf, nd):
    cap = idx.shape[-1]
    d = h.shape[1]
    nbg = 2 if bsz % 2 == 0 else 1
    nh = bsz // nbg
    rows = nbg * cap
    tf, tn = EXPERT_FF // nf, d // nd
    gidx = idx + (jnp.arange(bsz, dtype=I32) * t)[:, None, None]
    gidx = gidx.reshape(nh, nbg, N_EXPERTS, cap).transpose(2, 0, 1, 3).reshape(N_EXPERTS * nh, 1, rows)
    last = N_EXPERTS * nh - 1
    return pl.pallas_call(
        functools.partial(_ffn_kernel, nf, nd, nbg, cap),
        grid=(N_EXPERTS, nh, nf + nd),
        in_specs=[pl.BlockSpec((1, 1, rows), lambda e, hf, s: (e * nh + hf, 0, 0),
                               memory_space=pltpu.SMEM),
                  pl.BlockSpec((1, 1, rows), lambda e, hf, s: (jnp.minimum(e * nh + hf + 1, last), 0, 0),
                               memory_space=pltpu.SMEM),
                  pl.BlockSpec(memory_space=pl.ANY),
                  pl.BlockSpec((None, 1, d, tf), lambda e, hf, s: (layer, e, 0, jnp.minimum(s, nf - 1))),
                  pl.BlockSpec((None, 1, d, tf), lambda e, hf, s: (layer, e, 0, jnp.minimum(s, nf - 1))),
                  pl.BlockSpec((None, 1, EXPERT_FF, tn),
                               lambda e, hf, s: (layer, e, 0, jnp.maximum(s - nf, 0)))],
        out_specs=pl.BlockSpec((nbg, 1, cap, tn), lambda e, hf, s: (hf, e, 0, jnp.maximum(s - nf, 0))),
        out_shape=jax.ShapeDtypeStruct((bsz, N_EXPERTS, cap, d), BF16),
        scratch_shapes=[pltpu.VMEM((rows, d), F32), pltpu.VMEM((rows, d), BF16),
                        pltpu.VMEM((nf, rows, tf), BF16), pltpu.SemaphoreType.DMA],
        compiler_params=_cparams(("arbitrary", "arbitrary", "arbitrary"), V7X_VMEM_LIMIT_BYTES),
        name="moe_ffn",
    )(gidx, gidx, h, w_gate, w_up, w_down)


def _combine_kernel(cap, tblk, starts_ref, x_ref, pos_ref, w_ref, y_hbm, o_ref, kbuf, sems):
    b = pl.program_id(0)
    tb = pl.program_id(1)
    ntb = pl.num_programs(1)
    step = b * ntb + tb
    cur = step % 2
    o_ref[0] = x_ref[0]

    def firsts(bb, tt):
        return [(starts_ref[bb, tt, e] // 16) * 16 for e in range(N_EXPERTS)]

    def offsets(first, p):
        lows = [first[e] + p * ROW_WIN for e in range(N_EXPERTS)]
        return lows, [pl.multiple_of(jnp.minimum(lo, cap - ROW_WIN), 16) for lo in lows]

    def window_copy(bb, e, off, buf):
        return pltpu.make_async_copy(y_hbm.at[bb, e, pl.ds(off, ROW_WIN), :],
                                     kbuf.at[buf, pl.ds(e * ROW_WIN, ROW_WIN), :], sems.at[buf])

    def start_pass(bb, first, p, buf):
        _, offs = offsets(first, p)
        for e in range(N_EXPERTS):
            window_copy(bb, e, offs[e], buf).start()

    def finish_pass(first, p, buf):
        lows, offs = offsets(first, p)
        lane = lax.broadcasted_iota(I32, (tblk, V7X_LANES), 1)
        slot = lane % ROW_WIN
        head = lane < ROW_WIN
        tiles = []
        for g in range(N_EXPERTS * ROW_WIN // V7X_LANES):
            ea, eb = 2 * g, 2 * g + 1
            pe = jnp.where(head, pos_ref[0, :, ea:ea + 1], pos_ref[0, :, eb:eb + 1])
            we = jnp.where(head, w_ref[0, :, ea:ea + 1], w_ref[0, :, eb:eb + 1])
            off = jnp.where(head, offs[ea], offs[eb])
            low = jnp.where(head, lows[ea], lows[eb])
            hit = (pe == off + slot) & (pe >= low)
            tiles.append(jnp.where(hit, we, 0.0).astype(BF16))
        pmat = jnp.concatenate(tiles, axis=1)
        for e in range(N_EXPERTS):
            window_copy(b, e, offs[e], buf).wait()
        o_ref[0] += jnp.dot(pmat, kbuf[buf], preferred_element_type=F32)

    first = firsts(b, tb)
    npass = jnp.int32(1)
    for e in range(N_EXPERTS):
        npass = jnp.maximum(npass, (starts_ref[b, tb + 1, e] - first[e] + ROW_WIN - 1) // ROW_WIN)

    @pl.when(step == 0)
    def _():
        start_pass(b, first, 0, cur)

    @pl.when(step + 1 < pl.num_programs(0) * ntb)
    def _():
        wrap = tb + 1 == ntb
        nb = jnp.where(wrap, b + 1, b)
        nt = jnp.where(wrap, 0, tb + 1)
        start_pass(nb, firsts(nb, nt), 0, 1 - cur)

    finish_pass(first, 0, cur)

    def extra_pass(p, carry):
        start_pass(b, first, p, cur)
        finish_pass(first, p, cur)
        return carry

    lax.fori_loop(1, npass, extra_pass, 0)


def _combine(x, pos, wsel, starts, y, tblk):
    bsz, t, d = x.shape
    cap = y.shape[2]
    return pl.pallas_call(
        functools.partial(_combine_kernel, cap, tblk),
        grid=(bsz, t // tblk),
        in_specs=[pl.BlockSpec(memory_space=pltpu.SMEM),
                  pl.BlockSpec((1, tblk, d), lambda b, i: (b, i, 0)),
                  pl.BlockSpec((1, tblk, V7X_LANES), lambda b, i: (b, i, 0)),
                  pl.BlockSpec((1, tblk, V7X_LANES), lambda b, i: (b, i, 0)),
                  pl.BlockSpec(memory_space=pl.ANY)],
        out_specs=pl.BlockSpec((1, tblk, d), lambda b, i: (b, i, 0)),
        out_shape=jax.ShapeDtypeStruct((bsz, t, d), F32),
        scratch_shapes=[pltpu.VMEM((2, N_EXPERTS * ROW_WIN, d), BF16), pltpu.SemaphoreType.DMA((2,))],
        compiler_params=_cparams(("arbitrary", "arbitrary")),
        name="moe_combine",
    )(starts, x, pos, wsel, y)


def _tiles(n, t):
    pick = lambda want, total: want if total % want == 0 else total
    return dict(
        tm_proj=pick(1024, n), tn_proj=768,
        tq=pick(512, t), tk=pick(1024, t), lb=pick(512, t),
        tm_gsum=pick(1024, n), tn_gsum=256, tm_oproj=pick(512, n),
        nf_ffn=4, nd_ffn=4,
        tblk=pick(256, t),
    )


def kernel(x, positions, norm_mix, w_in, conv_a, q_norm, k_norm, lambda_q1, lambda_k1, lambda_q2, lambda_k2, subln, conv_c, a_log_f, a_log_b, dt_bias_f, dt_bias_b, o_norm, w_out_a, w_out_b, w_out_c, w_gate, b_gate, w_o, norm_ffn, w_router, w_e_gate, w_e_up, w_e_down):
    bsz, t, d = x.shape
    n = bsz * t
    depth = w_in.shape[0]
    ts = _tiles(n, t)
    tables = _rope_tables(positions)
    pad_small = jnp.zeros((d, V7X_LANES - N_SMALL), F32)
    tile2 = lambda g: jnp.concatenate([g, g]).reshape(1, V7X_LANES)
    pad64 = lambda vv: jnp.concatenate([vv, jnp.zeros_like(vv)])

    xf = x.reshape(n, d)
    for l in range(depth):
        lam_init = 0.8 - 0.6 * math.exp(-0.3 * l)
        w_small = jnp.concatenate([w_in[l, :, N_MAIN:], pad_small], axis=1)
        main, small = _norm_matmul(xf, norm_mix[l], w_in, l, N_MAIN, w_small,
                                   ts["tm_proj"], ts["tn_proj"])
        main3 = main.reshape(bsz, t, N_MAIN)
        small3 = small.reshape(bsz, t, V7X_LANES)

        mix_a = _mixer_a(main3, conv_a[l])
        lamv = jnp.stack([pad64(lambda_q1[l]), pad64(lambda_k1[l]),
                          pad64(lambda_q2[l]), pad64(lambda_k2[l])])
        mix_b = _diff_attention(main3, tables, tile2(q_norm[l]), tile2(k_norm[l]), lamv,
                                subln[l].reshape(1, V7X_LANES), lam_init, ts["tq"], ts["tk"])
        qkv_c = _gdn_prep(main3, conv_c[l])
        o_f = _gdn_direction(qkv_c, small3, a_log_f[l], dt_bias_f[l], False, ts["lb"])
        o_b = _gdn_direction(qkv_c, small3, a_log_b[l], dt_bias_b[l], True, ts["lb"])

        w_out = jnp.concatenate([w_out_a[l], w_out_b[l], w_out_c[l]], axis=0).astype(BF16)
        gsum = _gated_sum(xf, norm_mix[l], mix_a.reshape(n, W_A), mix_b.reshape(n, W_B),
                          o_f.reshape(n, W_C), o_b.reshape(n, W_C), main, o_norm[l],
                          w_out, w_gate[l].astype(BF16), b_gate[l].reshape(1, N_BRANCH * d),
                          ts["tm_gsum"], ts["tn_gsum"])
        x1 = _out_proj(xf, gsum, w_o[l].astype(BF16), ts["tm_oproj"])
        h, aff = _router(x1, norm_ffn[l], w_router[l], ts["tm_oproj"])
        idx, pos, wsel, starts = _select(aff, bsz, t, t // ts["tblk"])
        y = _expert_ffn(idx, h, w_e_gate, w_e_up, w_e_down, l, bsz, t, ts["nf_ffn"], ts["nd_ffn"])
        xf = _combine(x1.reshape(bsz, t, d), pos, wsel, starts, y, ts["tblk"]).reshape(n, d)
    return xf.reshape(bsz, t, d)
```

```python
import functools
import math

import jax
import jax.numpy as jnp
from jax import lax
from jax.experimental import pallas as pl
from jax.experimental.pallas import tpu as pltpu

F32 = jnp.float32
BF16 = jnp.bfloat16
I32 = jnp.int32

D_MODEL = 2048
HEAD_DIM = 128
W_A = D_MODEL // 4
DH_B = 64
DV_B = 2 * DH_B
H_B = (3 * D_MODEL // 8) // DV_B
W_B = H_B * DV_B
ROT_DIM = DH_B // 4
ROPE_THETA = 500000.0
DK_C = HEAD_DIM
DV_C = HEAD_DIM
H_C = (3 * D_MODEL // 8) // DV_C
W_C = H_C * DV_C
CHUNK = 64
N_BRANCH = 3
N_EXPERTS = 16
CAPACITY_FACTOR = 2
EXPERT_FF = D_MODEL // 2
EPS = 1e-6

N_MAIN = 3 * W_A + 3 * W_B + 4 * W_C
N_SMALL = 4 * H_C
COL_A = 0
COL_B = 3 * W_A
COL_C = COL_B + 3 * W_B

V7X_LANES = 128
BF16_SUBLANES = 16
V7X_VMEM_LIMIT_BYTES = 56 * 1024 * 1024
ROW_WIN = 64
NEG_POS = -(1 << 20)


def _cparams(sem, vmem=None):
    return pltpu.CompilerParams(dimension_semantics=sem, vmem_limit_bytes=vmem)


def _split_dot(a, b):
    hi = a.astype(BF16)
    lo = (a - hi.astype(F32)).astype(BF16)
    return (jnp.dot(hi, b, preferred_element_type=F32)
            + jnp.dot(lo, b, preferred_element_type=F32))


def _norm_matmul_kernel(x_ref, g_ref, w_ref, ws_ref, o_ref, os_ref, xn_ref):
    @pl.when(pl.program_id(1) == 0)
    def _():
        x = x_ref[...]
        xn = x * lax.rsqrt(jnp.mean(x * x, axis=-1, keepdims=True) + EPS) * g_ref[...]
        xn_ref[...] = xn.astype(BF16)
        os_ref[...] = jnp.dot(xn_ref[...], ws_ref[...].astype(BF16), preferred_element_type=F32)

    o_ref[...] = jnp.dot(xn_ref[...], w_ref[...], preferred_element_type=F32).astype(o_ref.dtype)


def _norm_matmul(x, gain, w, layer, nout, w_small, tm, tn):
    n, d = x.shape
    return pl.pallas_call(
        _norm_matmul_kernel,
        grid=(n // tm, nout // tn),
        in_specs=[pl.BlockSpec((tm, d), lambda i, j: (i, 0)),
                  pl.BlockSpec((1, d), lambda i, j: (0, 0)),
                  pl.BlockSpec((None, d, tn), lambda i, j: (layer, 0, j)),
                  pl.BlockSpec((d, V7X_LANES), lambda i, j: (0, 0))],
        out_specs=[pl.BlockSpec((tm, tn), lambda i, j: (i, j)),
                   pl.BlockSpec((tm, V7X_LANES), lambda i, j: (i, 0))],
        out_shape=[jax.ShapeDtypeStruct((n, nout), BF16),
                   jax.ShapeDtypeStruct((n, V7X_LANES), F32)],
        scratch_shapes=[pltpu.VMEM((tm, d), BF16)],
        compiler_params=_cparams(("parallel", "arbitrary"), V7X_VMEM_LIMIT_BYTES),
        name="norm_matmul",
    )(x, gain.reshape(1, d), w, w_small)


def _conv3(x, w):
    t = x.shape[0]
    row = lax.broadcasted_iota(I32, x.shape, 0)
    prev = jnp.where(row == 0, 0.0, pltpu.roll(x, 1, 0))
    nxt = jnp.where(row == t - 1, 0.0, pltpu.roll(x, t - 1, 0))
    return prev * w[0:1, :] + x * w[1:2, :] + nxt * w[2:3, :]


def _mixer_a_kernel(b_ref, c_ref, v_ref, w_ref, o_ref):
    cv = c_ref[0].astype(F32) * v_ref[0].astype(F32)
    o_ref[0] = (b_ref[0].astype(F32) * _conv3(cv, w_ref[...])).astype(o_ref.dtype)


def _mixer_a(main, conv_w):
    bsz, t, _ = main.shape
    nblk = W_A // V7X_LANES
    c0 = COL_A // V7X_LANES
    spec = lambda off: pl.BlockSpec((1, t, V7X_LANES), lambda b, c: (b, 0, c0 + off + c))
    return pl.pallas_call(
        _mixer_a_kernel,
        grid=(bsz, nblk),
        in_specs=[spec(0), spec(nblk), spec(2 * nblk),
                  pl.BlockSpec((3, V7X_LANES), lambda b, c: (0, c))],
        out_specs=pl.BlockSpec((1, t, V7X_LANES), lambda b, c: (b, 0, c)),
        out_shape=jax.ShapeDtypeStruct((bsz, t, W_A), BF16),
        compiler_params=_cparams(("parallel", "parallel")),
        name="mixer_a",
    )(main, main, main, conv_w)


def _gdn_prep_kernel(x_ref, w_ref, o_ref):
    c = pl.program_id(1)
    y = _conv3(x_ref[0].astype(F32), w_ref[...])
    y = y * jax.nn.sigmoid(y)
    inv = lax.rsqrt(jnp.sum(y * y, axis=-1, keepdims=True) + EPS)
    scale = jnp.where(c < H_C, inv * (DK_C ** -0.5), jnp.where(c < 2 * H_C, inv, 1.0))
    o_ref[0] = (y * scale).astype(o_ref.dtype)


def _gdn_prep(main, conv_w):
    bsz, t, _ = main.shape
    nblk = 3 * W_C // V7X_LANES
    c0 = COL_C // V7X_LANES
    return pl.pallas_call(
        _gdn_prep_kernel,
        grid=(bsz, nblk),
        in_specs=[pl.BlockSpec((1, t, V7X_LANES), lambda b, c: (b, 0, c0 + c)),
                  pl.BlockSpec((3, V7X_LANES), lambda b, c: (0, c))],
        out_specs=pl.BlockSpec((1, t, V7X_LANES), lambda b, c: (b, 0, c)),
        out_shape=jax.ShapeDtypeStruct((bsz, t, 3 * W_C), BF16),
        compiler_params=_cparams(("parallel", "parallel")),
        name="gdn_prep",
    )(main, conv_w)


def _rope_kernel(pos_ref, c_ref, s1_ref, s2_ref):
    pos = pos_ref[0].astype(F32)
    lane = lax.broadcasted_iota(I32, (1, V7X_LANES), 1)
    l64 = lane % DH_B
    half = ROT_DIM // 2
    fidx = jnp.where(l64 < half, l64, l64 - half).astype(F32)
    inv_freq = jnp.exp(fidx * (-math.log(ROPE_THETA) * 2.0 / ROT_DIM))
    ang = pos * inv_freq
    cos, sin = jnp.cos(ang), jnp.sin(ang)
    first = l64 < half
    second = (l64 >= half) & (l64 < ROT_DIM)
    c_ref[0] = jnp.where(first | second, cos, 1.0)
    s1_ref[0] = jnp.where(first, -sin, 0.0)
    s2_ref[0] = jnp.where(second, sin, 0.0)


def _rope_tables(positions):
    bsz, t = positions.shape
    shp = jax.ShapeDtypeStruct((bsz, t, V7X_LANES), F32)
    spec = pl.BlockSpec((1, t, V7X_LANES), lambda b: (b, 0, 0))
    return pl.pallas_call(
        _rope_kernel,
        grid=(bsz,),
        in_specs=[pl.BlockSpec((1, t, 1), lambda b: (b, 0, 0))],
        out_specs=[spec, spec, spec],
        out_shape=[shp, shp, shp],
        compiler_params=_cparams(("parallel",)),
        name="rope_tables",
    )(positions.reshape(bsz, t, 1))


def _qk_prep(x, gain, cos, s1, s2):
    r = lax.broadcasted_iota(I32, (V7X_LANES, V7X_LANES), 0) // DH_B
    c = lax.broadcasted_iota(I32, (V7X_LANES, V7X_LANES), 1) // DH_B
    blockdiag = (r == c).astype(BF16)
    ssq = _split_dot(x * x, blockdiag)
    xn = x * lax.rsqrt(ssq * (1.0 / DH_B) + EPS) * gain
    half = ROT_DIM // 2
    return xn * cos + pltpu.roll(xn, V7X_LANES - half, 1) * s1 + pltpu.roll(xn, half, 1) * s2


def _attn_kernel(lam_init, tq, tk, q_ref, k_ref, v_ref, c_ref, s1_ref, s2_ref,
                 qg_ref, kg_ref, lamv_ref, sub_ref, o_ref, kp_ref, vx_ref, acc_ref):
    qi = pl.program_id(2)
    t = kp_ref.shape[0]

    @pl.when(qi == 0)
    def _():
        kp_ref[...] = _qk_prep(k_ref[0].astype(F32), kg_ref[...],
                               c_ref[0], s1_ref[0], s2_ref[0]).astype(BF16)
        vx_ref[:DV_B, :] = v_ref[0].astype(F32).T.astype(BF16)
        vx_ref[DV_B:, :] = jnp.ones((vx_ref.shape[0] - DV_B, t), BF16)

    rows = pl.ds(pl.multiple_of(qi * tq, tq), tq)
    q = _qk_prep(q_ref[0].astype(F32), qg_ref[...], c_ref[0, rows, :], s1_ref[0, rows, :],
                 s2_ref[0, rows, :]) * (DH_B ** -0.5 * math.log2(math.e))
    lane = lax.broadcasted_iota(I32, (tq, V7X_LANES), 1)
    qt = [jnp.where(keep, q, 0.0).T.astype(BF16) for keep in (lane < DH_B, lane >= DH_B)]

    def scores(j):
        kj = kp_ref[j * tk:(j + 1) * tk, :]
        return [jnp.dot(kj, qc, preferred_element_type=F32) for qc in qt]

    acc_ref[...] = jnp.zeros_like(acc_ref)
    m = [jnp.full((1, tq), -jnp.inf, F32)] * 2
    nkb = t // tk
    ahead = 2
    pending = [scores(j) for j in range(min(ahead, nkb))]
    for j in range(nkb):
        cur = pending.pop(0)
        if j + ahead < nkb:
            pending.append(scores(j + ahead))
        vj = vx_ref[:, j * tk:(j + 1) * tk]
        for c in range(2):
            m_new = jnp.maximum(m[c], jnp.max(cur[c], axis=0, keepdims=True))
            e = jnp.exp2(cur[c] - m_new).astype(BF16)
            acc_ref[c] = acc_ref[c] * jnp.exp2(m[c] - m_new) + jnp.dot(vj, e, preferred_element_type=F32)
            m[c] = m_new
    pv = [acc_ref[c, :DV_B, :] for c in range(2)]
    l = [acc_ref[c, DV_B:DV_B + 1, :] for c in range(2)]
    lv = lamv_ref[...]
    lam = (jnp.exp(jnp.sum(lv[0:1] * lv[1:2], axis=-1, keepdims=True))
           - jnp.exp(jnp.sum(lv[2:3] * lv[3:4], axis=-1, keepdims=True)) + lam_init)
    o = (pv[0] / l[0] - lam * (pv[1] / l[1])).T
    o = o * lax.rsqrt(jnp.mean(o * o, axis=-1, keepdims=True) + EPS) * sub_ref[...]
    o_ref[0] = (o * (1.0 - lam_init)).astype(o_ref.dtype)


def _diff_attention(main, tables, q_gain, k_gain, lamv, subln, lam_init, tq, tk):
    bsz, t, _ = main.shape
    cq = COL_B // V7X_LANES
    ck = cq + H_B
    cv = ck + H_B
    tab = pl.BlockSpec((1, t, V7X_LANES), lambda b, h, i: (b, 0, 0))
    vec = pl.BlockSpec((1, V7X_LANES), lambda b, h, i: (0, 0))
    return pl.pallas_call(
        functools.partial(_attn_kernel, lam_init, tq, tk),
        grid=(bsz, H_B, t // tq),
        in_specs=[pl.BlockSpec((1, tq, V7X_LANES), lambda b, h, i: (b, i, cq + h)),
                  pl.BlockSpec((1, t, V7X_LANES), lambda b, h, i: (b, 0, ck + h)),
                  pl.BlockSpec((1, t, V7X_LANES), lambda b, h, i: (b, 0, cv + h)),
                  tab, tab, tab, vec, vec,
                  pl.BlockSpec((4, V7X_LANES), lambda b, h, i: (0, 0)),
                  vec],
        out_specs=pl.BlockSpec((1, tq, V7X_LANES), lambda b, h, i: (b, i, h)),
        out_shape=jax.ShapeDtypeStruct((bsz, t, W_B), BF16),
        scratch_shapes=[pltpu.VMEM((t, V7X_LANES), BF16),
                        pltpu.VMEM((DV_B + BF16_SUBLANES, t), BF16),
                        pltpu.VMEM((2, DV_B + BF16_SUBLANES, tq), F32)],
        compiler_params=_cparams(("parallel", "parallel", "arbitrary"), V7X_VMEM_LIMIT_BYTES),
        name="diff_attention",
    )(main, main, main, *tables, q_gain, k_gain, lamv, subln)


def _gdn_kernel(rev, lb, q_ref, k_ref, v_ref, sm_ref, alog_ref, dtb_ref, o_ref, s_ref):
    @pl.when(pl.program_id(1) == 0)
    def _():
        s_ref[...] = jnp.zeros_like(s_ref)

    off_b = H_C if rev else 0
    off_a = 3 * H_C if rev else 2 * H_C
    ri = lax.broadcasted_iota(I32, (CHUNK, CHUNK), 0)
    ci = lax.broadcasted_iota(I32, (CHUNK, CHUNK), 1)
    incl = (ri <= ci) if rev else (ri >= ci)
    strict = (ri < ci) if rev else (ri > ci)
    cum_mat = incl.astype(BF16)
    eye = (ri == ci).astype(F32)
    last = 0 if rev else CHUNK - 1

    nchunk = lb // CHUNK
    order = list(range(nchunk - 1, -1, -1) if rev else range(nchunk))
    units = [(c, h) for c in order for h in range(H_C)]
    nt = (((1,), (1,)), ((), ()))
    dot = functools.partial(jnp.dot, preferred_element_type=F32)

    gates = {}
    for c in order:
        rows = slice(c * CHUNK, (c + 1) * CHUNK)
        sm = sm_ref[0, rows, :]
        g_all = -jnp.exp(alog_ref[...]) * jax.nn.softplus(sm + dtb_ref[...])
        gc_all = _split_dot_left(cum_mat, g_all)
        gates[c] = (jax.nn.sigmoid(sm), gc_all, gc_all.T)

    pre = {}
    for c, h in units:
        rows = slice(c * CHUNK, (c + 1) * CHUNK)
        lanes = slice(h * HEAD_DIM, (h + 1) * HEAD_DIM)
        beta_all, gc_all, gct_all = gates[c]
        q = q_ref[0, rows, lanes].astype(F32)
        k = k_ref[0, rows, lanes]
        kf = k.astype(F32)
        beta = beta_all[:, off_b + h:off_b + h + 1]
        gcol = gc_all[:, off_a + h:off_a + h + 1]
        grow = gct_all[off_a + h:off_a + h + 1, :]
        glast = gcol[last:last + 1, :]
        decay = jnp.where(incl, jnp.exp(jnp.where(incl, gcol - grow, 0.0)), 0.0)
        kb = kf * beta
        eg = jnp.exp(gcol)
        rhs = jnp.concatenate([v_ref[0, rows, lanes].astype(F32) * beta, kb * eg], axis=-1).astype(BF16)
        pre[c, h] = dict(k=k, kb=kb, decay=decay, rhs=rhs, glast=glast,
                         q=q, q_dec=q * eg, k_dec=(kf * jnp.exp(glast - gcol)).astype(BF16))

    for u in units:
        d = pre[u]
        res = lax.dot_general(jnp.concatenate([d["kb"], d["q"]], axis=0).astype(BF16), d["k"], nt,
                              preferred_element_type=F32)
        d["low"] = jnp.where(strict, res[:CHUNK] * d["decay"], 0.0)
        d["attn"] = (res[CHUNK:] * d["decay"]).astype(BF16)
    for u in units:
        d = pre[u]
        lb16 = d["low"].astype(BF16)
        d["n"] = dot(lb16, lb16)
        d["p"] = eye - d["low"]
    for r in range(1, 6):
        for u in units:
            d = pre[u]
            nb16 = d["n"].astype(BF16)
            if r < 5:
                res = dot(jnp.concatenate([d["n"], d["p"]], axis=0).astype(BF16), nb16)
                d["n"] = res[:CHUNK]
                d["p"] = d["p"] + res[CHUNK:]
            else:
                d["p"] = d["p"] + dot(d["p"].astype(BF16), nb16)
    for u in units:
        d = pre[u]
        d["sol"] = dot(d["p"].astype(BF16), d["rhs"])

    st = {h: s_ref[h] for h in range(H_C)}
    for c in order:
        rows = slice(c * CHUNK, (c + 1) * CHUNK)
        res, vb = {}, {}
        for h in range(H_C):
            d = pre[c, h]
            lhs = jnp.concatenate([d["sol"][:, DV_C:], d["q_dec"]], axis=0).astype(BF16)
            res[h] = dot(lhs, st[h].astype(BF16))
        for h in range(H_C):
            vb[h] = (pre[c, h]["sol"][:, :DV_C] - res[h][:CHUNK]).astype(BF16)
        for h in range(H_C):
            d = pre[c, h]
            lanes = slice(h * HEAD_DIM, (h + 1) * HEAD_DIM)
            o_ref[0, rows, lanes] = (res[h][CHUNK:] + dot(d["attn"], vb[h])).astype(o_ref.dtype)
            st[h] = st[h] * jnp.exp(d["glast"]) + lax.dot_general(
                d["k_dec"], vb[h], (((0,), (0,)), ((), ())), preferred_element_type=F32)
    for h in range(H_C):
        s_ref[h] = st[h]


def _split_dot_left(a, b):
    hi = b.astype(BF16)
    lo = (b - hi.astype(F32)).astype(BF16)
    return (jnp.dot(a, hi, preferred_element_type=F32)
            + jnp.dot(a, lo, preferred_element_type=F32))


def _gdn_direction(qkv, small, a_log, dt_bias, rev, lb):
    bsz, t, _ = qkv.shape
    nb = t // lb
    blk = (lambda i: nb - 1 - i) if rev else (lambda i: i)
    off_a = 3 * H_C if rev else 2 * H_C
    pad = lambda vec: jnp.zeros((1, V7X_LANES), F32).at[0, off_a:off_a + H_C].set(vec)
    col = lambda j: pl.BlockSpec((1, lb, W_C), lambda b, i: (b, blk(i), j))
    vec = pl.BlockSpec((1, V7X_LANES), lambda b, i: (0, 0))
    return pl.pallas_call(
        functools.partial(_gdn_kernel, rev, lb),
        grid=(bsz, nb),
        in_specs=[col(0), col(1), col(2),
                  pl.BlockSpec((1, lb, V7X_LANES), lambda b, i: (b, blk(i), 0)),
                  vec, vec],
        out_specs=pl.BlockSpec((1, lb, W_C), lambda b, i: (b, blk(i), 0)),
        out_shape=jax.ShapeDtypeStruct((bsz, t, W_C), BF16),
        scratch_shapes=[pltpu.VMEM((H_C, DK_C, DV_C), F32)],
        compiler_params=_cparams(("parallel", "arbitrary")),
        name="gdn_bwd" if rev else "gdn_fwd",
    )(qkv, qkv, qkv, small, pad(a_log), pad(dt_bias))


def _gated_sum_kernel(x_ref, gn_ref, ma_ref, mb_ref, of_ref, ob_ref, gc_ref, og_ref,
                      wout_ref, wg0_ref, wg1_ref, wg2_ref, bg0_ref, bg1_ref, bg2_ref,
                      o_ref, xn_ref, mix_ref):
    j = pl.program_id(1)

    @pl.when(j == 0)
    def _():
        x = x_ref[...]
        xn = x * lax.rsqrt(jnp.mean(x * x, axis=-1, keepdims=True) + EPS) * gn_ref[...]
        xn_ref[...] = xn.astype(BF16)
        mix_ref[:, 0:W_A] = ma_ref[...]
        mix_ref[:, W_A:W_A + W_B] = mb_ref[...]
        for h in range(H_C):
            lanes = slice(h * HEAD_DIM, (h + 1) * HEAD_DIM)
            o = of_ref[:, lanes].astype(F32) + ob_ref[:, lanes].astype(F32)
            o = o * lax.rsqrt(jnp.mean(o * o, axis=-1, keepdims=True) + EPS) * og_ref[...]
            g = gc_ref[:, lanes].astype(F32)
            c0 = W_A + W_B + h * HEAD_DIM
            mix_ref[:, c0:c0 + HEAD_DIM] = (o * (g * jax.nn.sigmoid(g))).astype(BF16)

    xn = xn_ref[...]
    s = None
    for lo, hi, wg_ref, bg_ref in ((0, W_A, wg0_ref, bg0_ref),
                                   (W_A, W_A + W_B, wg1_ref, bg1_ref),
                                   (W_A + W_B, D_MODEL, wg2_ref, bg2_ref)):
        y = jnp.dot(mix_ref[:, lo:hi], wout_ref[lo:hi, :], preferred_element_type=F32)
        gate = jax.nn.sigmoid(jnp.dot(xn, wg_ref[...], preferred_element_type=F32) + bg_ref[...])
        s = gate * y if s is None else s + gate * y
    o_ref[...] = s.astype(o_ref.dtype)


def _gated_sum(x, norm_gain, mix_a, mix_b, o_f, o_b, main, o_gain, w_out, w_gate, b_gate, tm, tn):
    n, d = x.shape
    nj = d // tn
    cg = (COL_C + 3 * W_C) // W_C
    row = lambda w: pl.BlockSpec((tm, w), lambda i, j: (i, 0))
    wg = lambda br: pl.BlockSpec((d, tn), lambda i, j: (0, br * nj + j))
    bg = lambda br: pl.BlockSpec((1, tn), lambda i, j: (0, br * nj + j))
    return pl.pallas_call(
        _gated_sum_kernel,
        grid=(n // tm, nj),
        in_specs=[row(d),
                  pl.BlockSpec((1, d), lambda i, j: (0, 0)),
                  row(W_A), row(W_B), row(W_C), row(W_C),
                  pl.BlockSpec((tm, W_C), lambda i, j: (i, cg)),
                  pl.BlockSpec((1, HEAD_DIM), lambda i, j: (0, 0)),
                  pl.BlockSpec((d, tn), lambda i, j: (0, j)),
                  wg(0), wg(1), wg(2), bg(0), bg(1), bg(2)],
        out_specs=pl.BlockSpec((tm, tn), lambda i, j: (i, j)),
        out_shape=jax.ShapeDtypeStruct((n, d), BF16),
        scratch_shapes=[pltpu.VMEM((tm, d), BF16), pltpu.VMEM((tm, d), BF16)],
        compiler_params=_cparams(("parallel", "arbitrary"), V7X_VMEM_LIMIT_BYTES),
        name="gated_sum",
    )(x, norm_gain.reshape(1, d), mix_a, mix_b, o_f, o_b, main, o_gain.reshape(1, HEAD_DIM),
      w_out, w_gate, w_gate, w_gate, b_gate, b_gate, b_gate)


def _out_proj_kernel(x_ref, s_ref, w_ref, o_ref):
    o_ref[...] = x_ref[...] + jnp.dot(s_ref[...], w_ref[...], preferred_element_type=F32)


def _out_proj(x, s, w_o, tm):
    n, d = x.shape
    row = pl.BlockSpec((tm, d), lambda i: (i, 0))
    return pl.pallas_call(
        _out_proj_kernel,
        grid=(n // tm,),
        in_specs=[row, row, pl.BlockSpec((d, d), lambda i: (0, 0))],
        out_specs=row,
        out_shape=jax.ShapeDtypeStruct((n, d), F32),
        compiler_params=_cparams(("parallel",), V7X_VMEM_LIMIT_BYTES),
        name="out_proj",
    )(x, s, w_o)


def _router_kernel(x_ref, g_ref, wh_ref, wl_ref, h_ref, aff_ref):
    x = x_ref[...]
    h = x * lax.rsqrt(jnp.mean(x * x, axis=-1, keepdims=True) + EPS) * g_ref[...]
    h_ref[...] = h
    hh = h.astype(BF16)
    hl = (h - hh.astype(F32)).astype(BF16)
    logits = (jnp.dot(hh, wh_ref[...], preferred_element_type=F32)
              + jnp.dot(hl, wh_ref[...], preferred_element_type=F32)
              + jnp.dot(hh, wl_ref[...], preferred_element_type=F32))
    lane = lax.broadcasted_iota(I32, logits.shape, 1)
    valid = lane < N_EXPERTS
    logits = jnp.where(valid, logits, -jnp.inf)
    m = jnp.max(logits, axis=-1, keepdims=True)
    e = jnp.where(valid, jnp.exp(logits - m), 0.0)
    aff_ref[...] = e / jnp.sum(e, axis=-1, keepdims=True)


def _router(x, gain, w_router, tm):
    n, d = x.shape
    wpad = jnp.zeros((d, V7X_LANES), F32).at[:, :N_EXPERTS].set(w_router)
    w_hi = wpad.astype(BF16)
    w_lo = (wpad - w_hi.astype(F32)).astype(BF16)
    wspec = pl.BlockSpec((d, V7X_LANES), lambda i: (0, 0))
    return pl.pallas_call(
        _router_kernel,
        grid=(n // tm,),
        in_specs=[pl.BlockSpec((tm, d), lambda i: (i, 0)),
                  pl.BlockSpec((1, d), lambda i: (0, 0)), wspec, wspec],
        out_specs=[pl.BlockSpec((tm, d), lambda i: (i, 0)),
                   pl.BlockSpec((tm, V7X_LANES), lambda i: (i, 0))],
        out_shape=[jax.ShapeDtypeStruct((n, d), F32),
                   jax.ShapeDtypeStruct((n, V7X_LANES), F32)],
        compiler_params=_cparams(("parallel",), V7X_VMEM_LIMIT_BYTES),
        name="router",
    )(x, gain.reshape(1, d), w_hi, w_lo)


CUM_BLK = 256


def _select_kernel(cap, aff_ref, idx_ref, pos_ref, wsel_ref, starts_ref, cum_ref):
    t = aff_ref.shape[1]
    aff = aff_ref[0]

    def search(i, lo):
        cand = lo | jnp.left_shift(jnp.int32(1), 30 - i)
        cnt = jnp.sum((aff >= pltpu.bitcast(cand, F32)).astype(I32), axis=0, keepdims=True)
        return jnp.where(cnt >= cap, cand, lo)

    thr_bits = lax.fori_loop(0, 31, search, jnp.zeros((1, V7X_LANES), I32))
    thr = pltpu.bitcast(thr_bits, F32)
    gt = aff > thr
    eq = aff == thr
    need = (cap - jnp.sum(gt.astype(I32), axis=0, keepdims=True)).astype(F32)

    ri = lax.broadcasted_iota(I32, (CUM_BLK, CUM_BLK), 0)
    ci = lax.broadcasted_iota(I32, (CUM_BLK, CUM_BLK), 1)
    tri = (ri >= ci).astype(BF16)

    def cumsum_rows(mask_f32_of_block, store):
        off = jnp.zeros((1, V7X_LANES), F32)
        for blk in range(t // CUM_BLK):
            rows = slice(blk * CUM_BLK, (blk + 1) * CUM_BLK)
            c = jnp.dot(tri, mask_f32_of_block(rows).astype(BF16), preferred_element_type=F32) + off
            store(rows, c)
            off = c[CUM_BLK - 1:CUM_BLK, :]

    def store_cum(rows, c):
        cum_ref[rows, :] = c

    cumsum_rows(lambda rows: eq[rows].astype(F32), store_cum)
    sel = gt | (eq & (cum_ref[...] <= need))
    self32 = sel.astype(F32)
    cumsum_rows(lambda rows: self32[rows], store_cum)
    cum = cum_ref[...]
    pos_ref[0] = jnp.where(sel, cum.astype(I32) - 1, NEG_POS)
    wsel_ref[0] = jnp.where(sel, aff, 0.0)

    nstart = starts_ref.shape[1]
    tblk = t // (nstart - 1)
    starts_ref[0, 0:1, :] = jnp.zeros((1, V7X_LANES), I32)
    for tb in range(1, nstart):
        starts_ref[0, tb:tb + 1, :] = cum[tb * tblk - 1:tb * tblk, :].astype(I32)

    jj = lax.broadcasted_iota(I32, (1, cap), 1).astype(F32)
    ones = jnp.ones((8, CUM_BLK), BF16)
    for e in range(N_EXPERTS):
        acc = jnp.zeros((8, cap), F32)
        for blk in range(t // CUM_BLK):
            col = cum_ref[blk * CUM_BLK:(blk + 1) * CUM_BLK, e:e + 1]
            acc = acc + jnp.dot(ones, (col <= jj).astype(BF16), preferred_element_type=F32)
        idx_ref[0, e:e + 1, :] = acc[0:1, :].astype(I32)


def _select(aff, bsz, t, n_tok_blk):
    cap = CAPACITY_FACTOR * t // N_EXPERTS
    aff3 = aff.reshape(bsz, t, V7X_LANES)
    tl = pl.BlockSpec((1, t, V7X_LANES), lambda b: (b, 0, 0))
    return pl.pallas_call(
        functools.partial(_select_kernel, cap),
        grid=(bsz,),
        in_specs=[tl],
        out_specs=[pl.BlockSpec((1, N_EXPERTS, cap), lambda b: (b, 0, 0)), tl, tl,
                   pl.BlockSpec((1, n_tok_blk + 1, V7X_LANES), lambda b: (b, 0, 0))],
        out_shape=[jax.ShapeDtypeStruct((bsz, N_EXPERTS, cap), I32),
                   jax.ShapeDtypeStruct((bsz, t, V7X_LANES), I32),
                   jax.ShapeDtypeStruct((bsz, t, V7X_LANES), F32),
                   jax.ShapeDtypeStruct((bsz, n_tok_blk + 1, V7X_LANES), I32)],
        scratch_shapes=[pltpu.VMEM((t, V7X_LANES), F32)],
        compiler_params=_cparams(("parallel",), V7X_VMEM_LIMIT_BYTES),
        name="moe_select",
    )(aff3)


def _ffn_kernel(nf, nd, nbg, cap, cur_ref, nxt_ref, h_hbm, wg_ref, wu_ref, wd_ref, y_ref,
                xin, xb, hid, sem):
    s = pl.program_id(2)
    group = pl.program_id(0) * pl.num_programs(1) + pl.program_id(1)
    ngroups = pl.num_programs(0) * pl.num_programs(1)
    rows = nbg * cap
    per = rows // (nf + nd)
    tf = EXPERT_FF // nf

    def row_copy(idx_ref, j):
        return pltpu.make_async_copy(h_hbm.at[pl.ds(idx_ref[0, 0, j], 1), :],
                                     xin.at[pl.ds(j, 1), :], sem)

    def start_rows(idx_ref, base, n):
        def body(j, c):
            row_copy(idx_ref, base + j).start()
            return c
        lax.fori_loop(0, n, body, 0, unroll=8)

    def wait_rows(idx_ref, n):
        def body(j, c):
            row_copy(idx_ref, j).wait()
            return c
        lax.fori_loop(0, n, body, 0, unroll=8)

    @pl.when(s == 0)
    def _():
        @pl.when(group == 0)
        def _():
            start_rows(cur_ref, 0, rows)
        wait_rows(cur_ref, rows)
        xb[...] = xin[...].astype(BF16)

    def prefetch_rows():
        for j in range(per):
            row_copy(nxt_ref, s * per + j).start()

    @pl.when(s < nf)
    def _():
        prefetch_rows()
        x = xb[...]
        g = jnp.dot(x, wg_ref[0].astype(BF16), preferred_element_type=F32)
        u = jnp.dot(x, wu_ref[0].astype(BF16), preferred_element_type=F32)
        hid[s] = (g * jax.nn.sigmoid(g) * u).astype(BF16)

    @pl.when(s >= nf)
    def _():
        prefetch_rows()
        w = wd_ref[0].astype(BF16)
        acc = None
        for f in range(nf):
            part = jnp.dot(hid[f], w[f * tf:(f + 1) * tf, :], preferred_element_type=F32)
            acc = part if acc is None else acc + part
        for bi in range(nbg):
            y_ref[bi, 0] = acc[bi * cap:(bi + 1) * cap, :].astype(y_ref.dtype)

    @pl.when((group + 1 == ngroups) & (s + 1 == nf + nd))
    def _():
        wait_rows(nxt_ref, rows)


def _expert_ffn(idx, h, w_gate, w_up, w_down, layer, bsz, t, nf, nd):
    cap = idx.shape[-1]
    d = h.shape[1]
    nbg = 2 if bsz % 2 == 0 else 1
    nh = bsz // nbg
    rows = nbg * cap
    tf, tn = EXPERT_FF // nf, d // nd
    gidx = idx + (jnp.arange(bsz, dtype=I32) * t)[:, None, None]
    gidx = gidx.reshape(nh, nbg, N_EXPERTS, cap).transpose(2, 0, 1, 3).reshape(N_EXPERTS * nh, 1, rows)
    last = N_EXPERTS * nh - 1
    return pl.pallas_call(
        functools.partial(_ffn_kernel, nf, nd, nbg, cap),
        grid=(N_EXPERTS, nh, nf + nd),
        in_specs=[pl.BlockSpec((1, 1, rows), lambda e, hf, s: (e * nh + hf, 0, 0),
                               memory_space=pltpu.SMEM),
                  pl.BlockSpec((1, 1, rows), lambda e, hf, s: (jnp.minimum(e * nh + hf + 1, last), 0, 0),
                               memory_space=pltpu.SMEM),
                  pl.BlockSpec(memory_space=pl.ANY),
                  pl.BlockSpec((None, 1, d, tf), lambda e, hf, s: (layer, e, 0, jnp.minimum(s, nf - 1))),
                  pl.BlockSpec((None, 1, d, tf), lambda e, hf, s: (layer, e, 0, jnp.minimum(s, nf - 1))),
                  pl.BlockSpec((None, 1, EXPERT_FF, tn),
                               lambda e, hf, s: (layer, e, 0, jnp.maximum(s - nf, 0)))],
        out_specs=pl.BlockSpec((nbg, 1, cap, tn), lambda e, hf, s: (hf, e, 0, jnp.maximum(s - nf, 0))),
        out_shape=jax.ShapeDtypeStruct((bsz, N_EXPERTS, cap, d), BF16),
        scratch_shapes=[pltpu.VMEM((rows, d), F32), pltpu.VMEM((rows, d), BF16),
                        pltpu.VMEM((nf, rows, tf), BF16), pltpu.SemaphoreType.DMA],
        compiler_params=_cparams(("arbitrary", "arbitrary", "arbitrary"), V7X_VMEM_LIMIT_BYTES),
        name="moe_ffn",
    )(gidx, gidx, h, w_gate, w_up, w_down)


def _combine_kernel(cap, tblk, starts_ref, x_ref, pos_ref, w_ref, y_hbm, o_ref, kbuf, sems):
    b = pl.program_id(0)
    tb = pl.program_id(1)
    ntb = pl.num_programs(1)
    step = b * ntb + tb
    cur = step % 2
    o_ref[0] = x_ref[0]

    def firsts(bb, tt):
        return [(starts_ref[bb, tt, e] // 16) * 16 for e in range(N_EXPERTS)]

    def offsets(first, p):
        lows = [first[e] + p * ROW_WIN for e in range(N_EXPERTS)]
        return lows, [pl.multiple_of(jnp.minimum(lo, cap - ROW_WIN), 16) for lo in lows]

    def window_copy(bb, e, off, buf):
        return pltpu.make_async_copy(y_hbm.at[bb, e, pl.ds(off, ROW_WIN), :],
                                     kbuf.at[buf, pl.ds(e * ROW_WIN, ROW_WIN), :], sems.at[buf])

    def start_pass(bb, first, p, buf):
        _, offs = offsets(first, p)
        for e in range(N_EXPERTS):
            window_copy(bb, e, offs[e], buf).start()

    def finish_pass(first, p, buf):
        lows, offs = offsets(first, p)
        lane = lax.broadcasted_iota(I32, (tblk, V7X_LANES), 1)
        slot = lane % ROW_WIN
        head = lane < ROW_WIN
        tiles = []
        for g in range(N_EXPERTS * ROW_WIN // V7X_LANES):
            ea, eb = 2 * g, 2 * g + 1
            pe = jnp.where(head, pos_ref[0, :, ea:ea + 1], pos_ref[0, :, eb:eb + 1])
            we = jnp.where(head, w_ref[0, :, ea:ea + 1], w_ref[0, :, eb:eb + 1])
            off = jnp.where(head, offs[ea], offs[eb])
            low = jnp.where(head, lows[ea], lows[eb])
            hit = (pe == off + slot) & (pe >= low)
            tiles.append(jnp.where(hit, we, 0.0).astype(BF16))
        pmat = jnp.concatenate(tiles, axis=1)
        for e in range(N_EXPERTS):
            window_copy(b, e, offs[e], buf).wait()
        o_ref[0] += jnp.dot(pmat, kbuf[buf], preferred_element_type=F32)

    first = firsts(b, tb)
    npass = jnp.int32(1)
    for e in range(N_EXPERTS):
        npass = jnp.maximum(npass, (starts_ref[b, tb + 1, e] - first[e] + ROW_WIN - 1) // ROW_WIN)

    @pl.when(step == 0)
    def _():
        start_pass(b, first, 0, cur)

    @pl.when(step + 1 < pl.num_programs(0) * ntb)
    def _():
        wrap = tb + 1 == ntb
        nb = jnp.where(wrap, b + 1, b)
        nt = jnp.where(wrap, 0, tb + 1)
        start_pass(nb, firsts(nb, nt), 0, 1 - cur)

    finish_pass(first, 0, cur)

    def extra_pass(p, carry):
        start_pass(b, first, p, cur)
        finish_pass(first, p, cur)
        return carry

    lax.fori_loop(1, npass, extra_pass, 0)


def _combine(x, pos, wsel, starts, y, tblk):
    bsz, t, d = x.shape
    cap = y.shape[2]
    return pl.pallas_call(
        functools.partial(_combine_kernel, cap, tblk),
        grid=(bsz, t // tblk),
        in_specs=[pl.BlockSpec(memory_space=pltpu.SMEM),
                  pl.BlockSpec((1, tblk, d), lambda b, i: (b, i, 0)),
                  pl.BlockSpec((1, tblk, V7X_LANES), lambda b, i: (b, i, 0)),
                  pl.BlockSpec((1, tblk, V7X_LANES), lambda b, i: (b, i, 0)),
                  pl.BlockSpec(memory_space=pl.ANY)],
        out_specs=pl.BlockSpec((1, tblk, d), lambda b, i: (b, i, 0)),
        out_shape=jax.ShapeDtypeStruct((bsz, t, d), F32),
        scratch_shapes=[pltpu.VMEM((2, N_EXPERTS * ROW_WIN, d), BF16), pltpu.SemaphoreType.DMA((2,))],
        compiler_params=_cparams(("arbitrary", "arbitrary")),
        name="moe_combine",
    )(starts, x, pos, wsel, y)


def _tiles(n, t):
    pick = lambda want, total: want if total % want == 0 else total
    return dict(
        tm_proj=pick(1024, n), tn_proj=768,
        tq=pick(512, t), tk=pick(1024, t), lb=pick(512, t),
        tm_gsum=pick(1024, n), tn_gsum=256, tm_oproj=pick(512, n),
        nf_ffn=4, nd_ffn=4,
        tblk=pick(256, t),
    )


def kernel(x, positions, norm_mix, w_in, conv_a, q_norm, k_norm, lambda_q1, lambda_k1, lambda_q2, lambda_k2, subln, conv_c, a_log_f, a_log_b, dt_bias_f, dt_bias_b, o_norm, w_out_a, w_out_b, w_out_c, w_gate, b_gate, w_o, norm_ffn, w_router, w_e_gate, w_e_up, w_e_down):
    bsz, t, d = x.shape
    n = bsz * t
    depth = w_in.shape[0]
    ts = _tiles(n, t)
    tables = _rope_tables(positions)
    pad_small = jnp.zeros((d, V7X_LANES - N_SMALL), F32)
    tile2 = lambda g: jnp.concatenate([g, g]).reshape(1, V7X_LANES)
    pad64 = lambda vv: jnp.concatenate([vv, jnp.zeros_like(vv)])
    w_in_bf = w_in.astype(BF16)

    xf = x.reshape(n, d)
    for l in range(depth):
        lam_init = 0.8 - 0.6 * math.exp(-0.3 * l)
        w_small = jnp.concatenate([w_in[l, :, N_MAIN:], pad_small], axis=1)
        main, small = _norm_matmul(xf, norm_mix[l], w_in_bf, l, N_MAIN, w_small,
                                   ts["tm_proj"], ts["tn_proj"])
        main3 = main.reshape(bsz, t, N_MAIN)
        small3 = small.reshape(bsz, t, V7X_LANES)

        mix_a = _mixer_a(main3, conv_a[l])
        lamv = jnp.stack([pad64(lambda_q1[l]), pad64(lambda_k1[l]),
                          pad64(lambda_q2[l]), pad64(lambda_k2[l])])
        mix_b = _diff_attention(main3, tables, tile2(q_norm[l]), tile2(k_norm[l]), lamv,
                                subln[l].reshape(1, V7X_LANES), lam_init, ts["tq"], ts["tk"])
        qkv_c = _gdn_prep(main3, conv_c[l])
        o_f = _gdn_direction(qkv_c, small3, a_log_f[l], dt_bias_f[l], False, ts["lb"])
        o_b = _gdn_direction(qkv_c, small3, a_log_b[l], dt_bias_b[l], True, ts["lb"])

        w_out = jnp.concatenate([w_out_a[l], w_out_b[l], w_out_c[l]], axis=0).astype(BF16)
        gsum = _gated_sum(xf, norm_mix[l], mix_a.reshape(n, W_A), mix_b.reshape(n, W_B),
                          o_f.reshape(n, W_C), o_b.reshape(n, W_C), main, o_norm[l],
                          w_out, w_gate[l].astype(BF16), b_gate[l].reshape(1, N_BRANCH * d),
                          ts["tm_gsum"], ts["tn_gsum"])
        x1 = _out_proj(xf, gsum, w_o[l].astype(BF16), ts["tm_oproj"])
        h, aff = _router(x1, norm_ffn[l], w_router[l], ts["tm_oproj"])
        idx, pos, wsel, starts = _select(aff, bsz, t, t // ts["tblk"])
        y = _expert_ffn(idx, h, w_e_gate, w_e_up, w_e_down, l, bsz, t, ts["nf_ffn"], ts["nd_ffn"])
        xf = _combine(x1.reshape(bsz, t, d), pos, wsel, starts, y, ts["tblk"]).reshape(n, d)
    return xf.reshape(bsz, t, d)
```

```python
import functools
import math

import jax
import jax.numpy as jnp
from jax import lax
from jax.experimental import pallas as pl
from jax.experimental.pallas import tpu as pltpu

F32 = jnp.float32
BF16 = jnp.bfloat16
I32 = jnp.int32

D_MODEL = 2048
HEAD_DIM = 128
W_A = D_MODEL // 4
DH_B = 64
DV_B = 2 * DH_B
H_B = (3 * D_MODEL // 8) // DV_B
W_B = H_B * DV_B
ROT_DIM = DH_B // 4
ROPE_THETA = 500000.0
DK_C = HEAD_DIM
DV_C = HEAD_DIM
H_C = (3 * D_MODEL // 8) // DV_C
W_C = H_C * DV_C
CHUNK = 64
N_BRANCH = 3
N_EXPERTS = 16
CAPACITY_FACTOR = 2
EXPERT_FF = D_MODEL // 2
EPS = 1e-6

N_MAIN = 3 * W_A + 3 * W_B + 4 * W_C
N_SMALL = 4 * H_C
COL_A = 0
COL_B = 3 * W_A
COL_C = COL_B + 3 * W_B

V7X_LANES = 128
BF16_SUBLANES = 16
V7X_VMEM_LIMIT_BYTES = 56 * 1024 * 1024
ROW_WIN = 64
NEG_POS = -(1 << 20)


def _cparams(sem, vmem=None):
    return pltpu.CompilerParams(dimension_semantics=sem, vmem_limit_bytes=vmem)


def _split_dot(a, b):
    hi = a.astype(BF16)
    lo = (a - hi.astype(F32)).astype(BF16)
    return (jnp.dot(hi, b, preferred_element_type=F32)
            + jnp.dot(lo, b, preferred_element_type=F32))


def _norm_matmul_kernel(x_ref, g_ref, w_ref, ws_ref, o_ref, os_ref, xn_ref):
    @pl.when(pl.program_id(1) == 0)
    def _():
        x = x_ref[...]
        xn = x * lax.rsqrt(jnp.mean(x * x, axis=-1, keepdims=True) + EPS) * g_ref[...]
        xn_ref[...] = xn.astype(BF16)
        os_ref[...] = jnp.dot(xn_ref[...], ws_ref[...].astype(BF16), preferred_element_type=F32)

    o_ref[...] = jnp.dot(xn_ref[...], w_ref[...], preferred_element_type=F32).astype(o_ref.dtype)


def _norm_matmul(x, gain, w, layer, nout, w_small, tm, tn):
    n, d = x.shape
    return pl.pallas_call(
        _norm_matmul_kernel,
        grid=(n // tm, nout // tn),
        in_specs=[pl.BlockSpec((tm, d), lambda i, j: (i, 0)),
                  pl.BlockSpec((1, d), lambda i, j: (0, 0)),
                  pl.BlockSpec((None, d, tn), lambda i, j: (layer, 0, j)),
                  pl.BlockSpec((d, V7X_LANES), lambda i, j: (0, 0))],
        out_specs=[pl.BlockSpec((tm, tn), lambda i, j: (i, j)),
                   pl.BlockSpec((tm, V7X_LANES), lambda i, j: (i, 0))],
        out_shape=[jax.ShapeDtypeStruct((n, nout), BF16),
                   jax.ShapeDtypeStruct((n, V7X_LANES), F32)],
        scratch_shapes=[pltpu.VMEM((tm, d), BF16)],
        compiler_params=_cparams(("parallel", "arbitrary"), V7X_VMEM_LIMIT_BYTES),
        name="norm_matmul",
    )(x, gain.reshape(1, d), w, w_small)


def _conv3(x, w):
    t = x.shape[0]
    row = lax.broadcasted_iota(I32, x.shape, 0)
    prev = jnp.where(row == 0, 0.0, pltpu.roll(x, 1, 0))
    nxt = jnp.where(row == t - 1, 0.0, pltpu.roll(x, t - 1, 0))
    return prev * w[0:1, :] + x * w[1:2, :] + nxt * w[2:3, :]


def _mixer_a_kernel(b_ref, c_ref, v_ref, w_ref, o_ref):
    cv = c_ref[0].astype(F32) * v_ref[0].astype(F32)
    o_ref[0] = (b_ref[0].astype(F32) * _conv3(cv, w_ref[...])).astype(o_ref.dtype)


def _mixer_a(main, conv_w):
    bsz, t, _ = main.shape
    nblk = W_A // V7X_LANES
    c0 = COL_A // V7X_LANES
    spec = lambda off: pl.BlockSpec((1, t, V7X_LANES), lambda b, c: (b, 0, c0 + off + c))
    return pl.pallas_call(
        _mixer_a_kernel,
        grid=(bsz, nblk),
        in_specs=[spec(0), spec(nblk), spec(2 * nblk),
                  pl.BlockSpec((3, V7X_LANES), lambda b, c: (0, c))],
        out_specs=pl.BlockSpec((1, t, V7X_LANES), lambda b, c: (b, 0, c)),
        out_shape=jax.ShapeDtypeStruct((bsz, t, W_A), BF16),
        compiler_params=_cparams(("parallel", "parallel")),
        name="mixer_a",
    )(main, main, main, conv_w)


def _gdn_prep_kernel(x_ref, w_ref, o_ref):
    c = pl.program_id(1)
    y = _conv3(x_ref[0].astype(F32), w_ref[...])
    y = y * jax.nn.sigmoid(y)
    inv = lax.rsqrt(jnp.sum(y * y, axis=-1, keepdims=True) + EPS)
    scale = jnp.where(c < H_C, inv * (DK_C ** -0.5), jnp.where(c < 2 * H_C, inv, 1.0))
    o_ref[0] = (y * scale).astype(o_ref.dtype)


def _gdn_prep(main, conv_w):
    bsz, t, _ = main.shape
    nblk = 3 * W_C // V7X_LANES
    c0 = COL_C // V7X_LANES
    return pl.pallas_call(
        _gdn_prep_kernel,
        grid=(bsz, nblk),
        in_specs=[pl.BlockSpec((1, t, V7X_LANES), lambda b, c: (b, 0, c0 + c)),
                  pl.BlockSpec((3, V7X_LANES), lambda b, c: (0, c))],
        out_specs=pl.BlockSpec((1, t, V7X_LANES), lambda b, c: (b, 0, c)),
        out_shape=jax.ShapeDtypeStruct((bsz, t, 3 * W_C), BF16),
        compiler_params=_cparams(("parallel", "parallel")),
        name="gdn_prep",
    )(main, conv_w)


def _rope_kernel(pos_ref, c_ref, s1_ref, s2_ref):
    pos = pos_ref[0].astype(F32)
    lane = lax.broadcasted_iota(I32, (1, V7X_LANES), 1)
    l64 = lane % DH_B
    half = ROT_DIM // 2
    fidx = jnp.where(l64 < half, l64, l64 - half).astype(F32)
    inv_freq = jnp.exp(fidx * (-math.log(ROPE_THETA) * 2.0 / ROT_DIM))
    ang = pos * inv_freq
    cos, sin = jnp.cos(ang), jnp.sin(ang)
    first = l64 < half
    second = (l64 >= half) & (l64 < ROT_DIM)
    c_ref[0] = jnp.where(first | second, cos, 1.0)
    s1_ref[0] = jnp.where(first, -sin, 0.0)
    s2_ref[0] = jnp.where(second, sin, 0.0)


def _rope_tables(positions):
    bsz, t = positions.shape
    shp = jax.ShapeDtypeStruct((bsz, t, V7X_LANES), F32)
    spec = pl.BlockSpec((1, t, V7X_LANES), lambda b: (b, 0, 0))
    return pl.pallas_call(
        _rope_kernel,
        grid=(bsz,),
        in_specs=[pl.BlockSpec((1, t, 1), lambda b: (b, 0, 0))],
        out_specs=[spec, spec, spec],
        out_shape=[shp, shp, shp],
        compiler_params=_cparams(("parallel",)),
        name="rope_tables",
    )(positions.reshape(bsz, t, 1))


def _qk_prep(x, gain, cos, s1, s2):
    r = lax.broadcasted_iota(I32, (V7X_LANES, V7X_LANES), 0) // DH_B
    c = lax.broadcasted_iota(I32, (V7X_LANES, V7X_LANES), 1) // DH_B
    blockdiag = (r == c).astype(BF16)
    ssq = _split_dot(x * x, blockdiag)
    xn = x * lax.rsqrt(ssq * (1.0 / DH_B) + EPS) * gain
    half = ROT_DIM // 2
    return xn * cos + pltpu.roll(xn, V7X_LANES - half, 1) * s1 + pltpu.roll(xn, half, 1) * s2


def _attn_kernel(lam_init, tq, tk, q_ref, k_ref, v_ref, c_ref, s1_ref, s2_ref,
                 qg_ref, kg_ref, lamv_ref, sub_ref, o_ref, kp_ref, vx_ref, acc_ref):
    qi = pl.program_id(2)
    t = kp_ref.shape[0]

    @pl.when(qi == 0)
    def _():
        kp_ref[...] = _qk_prep(k_ref[0].astype(F32), kg_ref[...],
                               c_ref[0], s1_ref[0], s2_ref[0]).astype(BF16)
        vx_ref[:DV_B, :] = v_ref[0].astype(F32).T.astype(BF16)
        vx_ref[DV_B:, :] = jnp.ones((vx_ref.shape[0] - DV_B, t), BF16)

    rows = pl.ds(pl.multiple_of(qi * tq, tq), tq)
    q = _qk_prep(q_ref[0].astype(F32), qg_ref[...], c_ref[0, rows, :], s1_ref[0, rows, :],
                 s2_ref[0, rows, :]) * (DH_B ** -0.5 * math.log2(math.e))
    lane = lax.broadcasted_iota(I32, (tq, V7X_LANES), 1)
    qt = [jnp.where(keep, q, 0.0).T.astype(BF16) for keep in (lane < DH_B, lane >= DH_B)]

    def scores(j):
        kj = kp_ref[j * tk:(j + 1) * tk, :]
        return [jnp.dot(kj, qc, preferred_element_type=F32) for qc in qt]

    acc_ref[...] = jnp.zeros_like(acc_ref)
    m = [jnp.full((1, tq), -jnp.inf, F32)] * 2
    nkb = t // tk
    ahead = 2
    pending = [scores(j) for j in range(min(ahead, nkb))]
    for j in range(nkb):
        cur = pending.pop(0)
        if j + ahead < nkb:
            pending.append(scores(j + ahead))
        vj = vx_ref[:, j * tk:(j + 1) * tk]
        for c in range(2):
            m_new = jnp.maximum(m[c], jnp.max(cur[c], axis=0, keepdims=True))
            e = jnp.exp2(cur[c] - m_new).astype(BF16)
            acc_ref[c] = acc_ref[c] * jnp.exp2(m[c] - m_new) + jnp.dot(vj, e, preferred_element_type=F32)
            m[c] = m_new
    pv = [acc_ref[c, :DV_B, :] for c in range(2)]
    l = [acc_ref[c, DV_B:DV_B + 1, :] for c in range(2)]
    lv = lamv_ref[...]
    lam = (jnp.exp(jnp.sum(lv[0:1] * lv[1:2], axis=-1, keepdims=True))
           - jnp.exp(jnp.sum(lv[2:3] * lv[3:4], axis=-1, keepdims=True)) + lam_init)
    o = (pv[0] / l[0] - lam * (pv[1] / l[1])).T
    o = o * lax.rsqrt(jnp.mean(o * o, axis=-1, keepdims=True) + EPS) * sub_ref[...]
    o_ref[0] = (o * (1.0 - lam_init)).astype(o_ref.dtype)


def _diff_attention(main, tables, q_gain, k_gain, lamv, subln, lam_init, tq, tk):
    bsz, t, _ = main.shape
    cq = COL_B // V7X_LANES
    ck = cq + H_B
    cv = ck + H_B
    tab = pl.BlockSpec((1, t, V7X_LANES), lambda b, h, i: (b, 0, 0))
    vec = pl.BlockSpec((1, V7X_LANES), lambda b, h, i: (0, 0))
    return pl.pallas_call(
        functools.partial(_attn_kernel, lam_init, tq, tk),
        grid=(bsz, H_B, t // tq),
        in_specs=[pl.BlockSpec((1, tq, V7X_LANES), lambda b, h, i: (b, i, cq + h)),
                  pl.BlockSpec((1, t, V7X_LANES), lambda b, h, i: (b, 0, ck + h)),
                  pl.BlockSpec((1, t, V7X_LANES), lambda b, h, i: (b, 0, cv + h)),
                  tab, tab, tab, vec, vec,
                  pl.BlockSpec((4, V7X_LANES), lambda b, h, i: (0, 0)),
                  vec],
        out_specs=pl.BlockSpec((1, tq, V7X_LANES), lambda b, h, i: (b, i, h)),
        out_shape=jax.ShapeDtypeStruct((bsz, t, W_B), BF16),
        scratch_shapes=[pltpu.VMEM((t, V7X_LANES), BF16),
                        pltpu.VMEM((DV_B + BF16_SUBLANES, t), BF16),
                        pltpu.VMEM((2, DV_B + BF16_SUBLANES, tq), F32)],
        compiler_params=_cparams(("parallel", "parallel", "arbitrary"), V7X_VMEM_LIMIT_BYTES),
        name="diff_attention",
    )(main, main, main, *tables, q_gain, k_gain, lamv, subln)


def _gdn_kernel(lb, qf_ref, kf_ref, vf_ref, smf_ref, qb_ref, kb_ref, vb_ref, smb_ref,
                alf_ref, dtf_ref, alb_ref, dtb_ref, of_ref, ob_ref, s_ref):
    @pl.when(pl.program_id(1) == 0)
    def _():
        s_ref[...] = jnp.zeros_like(s_ref)

    nchunk = lb // CHUNK
    ri = lax.broadcasted_iota(I32, (CHUNK, CHUNK), 0)
    ci = lax.broadcasted_iota(I32, (CHUNK, CHUNK), 1)
    eye = (ri == ci).astype(F32)
    sides = []
    for rev, refs in ((False, (qf_ref, kf_ref, vf_ref, smf_ref, alf_ref, dtf_ref, of_ref)),
                      (True, (qb_ref, kb_ref, vb_ref, smb_ref, alb_ref, dtb_ref, ob_ref))):
        incl = (ri <= ci) if rev else (ri >= ci)
        sides.append(dict(
            q=refs[0], k=refs[1], v=refs[2], sm=refs[3], alog=refs[4], dtb=refs[5], o=refs[6],
            incl=incl, strict=(ri < ci) if rev else (ri > ci), cum_mat=incl.astype(BF16),
            last=0 if rev else CHUNK - 1,
            off_b=H_C if rev else 0,
            off_a=3 * H_C if rev else 2 * H_C,
            order=list(range(nchunk - 1, -1, -1) if rev else range(nchunk))))
    units = [(si, sides[si]["order"][k], h) for k in range(nchunk) for si in range(2) for h in range(H_C)]
    nt = (((1,), (1,)), ((), ()))
    dot = functools.partial(jnp.dot, preferred_element_type=F32)

    gates = {}
    for si, sd in enumerate(sides):
        for c in sd["order"]:
            rows = slice(c * CHUNK, (c + 1) * CHUNK)
            sm = sd["sm"][0, rows, :]
            g_all = -jnp.exp(sd["alog"][...]) * jax.nn.softplus(sm + sd["dtb"][...])
            gc_all = _split_dot_left(sd["cum_mat"], g_all)
            gates[si, c] = (jax.nn.sigmoid(sm), gc_all, gc_all.T)

    pre = {}
    for si, c, h in units:
        sd = sides[si]
        rows = slice(c * CHUNK, (c + 1) * CHUNK)
        lanes = slice(h * HEAD_DIM, (h + 1) * HEAD_DIM)
        beta_all, gc_all, gct_all = gates[si, c]
        q = sd["q"][0, rows, lanes].astype(F32)
        k = sd["k"][0, rows, lanes]
        kf = k.astype(F32)
        beta = beta_all[:, sd["off_b"] + h:sd["off_b"] + h + 1]
        gcol = gc_all[:, sd["off_a"] + h:sd["off_a"] + h + 1]
        grow = gct_all[sd["off_a"] + h:sd["off_a"] + h + 1, :]
        glast = gcol[sd["last"]:sd["last"] + 1, :]
        decay = jnp.where(sd["incl"], jnp.exp(jnp.where(sd["incl"], gcol - grow, 0.0)), 0.0)
        kb = kf * beta
        eg = jnp.exp(gcol)
        rhs = jnp.concatenate([sd["v"][0, rows, lanes].astype(F32) * beta, kb * eg], axis=-1).astype(BF16)
        pre[si, c, h] = dict(k=k, kb=kb, decay=decay, rhs=rhs, glast=glast,
                             q=q, q_dec=q * eg, k_dec=(kf * jnp.exp(glast - gcol)).astype(BF16))

    for u in units:
        d = pre[u]
        res = lax.dot_general(jnp.concatenate([d["kb"], d["q"]], axis=0).astype(BF16), d["k"], nt,
                              preferred_element_type=F32)
        d["low"] = jnp.where(sides[u[0]]["strict"], res[:CHUNK] * d["decay"], 0.0)
        d["attn"] = (res[CHUNK:] * d["decay"]).astype(BF16)
    for u in units:
        d = pre[u]
        lb16 = d["low"].astype(BF16)
        d["n"] = dot(lb16, lb16)
        d["p"] = eye - d["low"]
    for r in range(1, 6):
        for u in units:
            d = pre[u]
            nb16 = d["n"].astype(BF16)
            if r < 5:
                res = dot(jnp.concatenate([d["n"], d["p"]], axis=0).astype(BF16), nb16)
                d["n"] = res[:CHUNK]
                d["p"] = d["p"] + res[CHUNK:]
            else:
                d["p"] = d["p"] + dot(d["p"].astype(BF16), nb16)
    for u in units:
        d = pre[u]
        d["sol"] = dot(d["p"].astype(BF16), d["rhs"])

    chains = [(si, h) for si in range(2) for h in range(H_C)]
    st = {ch: s_ref[ch[0], ch[1]] for ch in chains}
    for kk in range(nchunk):
        res, vb = {}, {}
        for si, h in chains:
            d = pre[si, sides[si]["order"][kk], h]
            lhs = jnp.concatenate([d["sol"][:, DV_C:], d["q_dec"]], axis=0).astype(BF16)
            res[si, h] = dot(lhs, st[si, h].astype(BF16))
        for si, h in chains:
            d = pre[si, sides[si]["order"][kk], h]
            vb[si, h] = (d["sol"][:, :DV_C] - res[si, h][:CHUNK]).astype(BF16)
        for si, h in chains:
            c = sides[si]["order"][kk]
            d = pre[si, c, h]
            o_ref = sides[si]["o"]
            o_ref[0, c * CHUNK:(c + 1) * CHUNK, h * HEAD_DIM:(h + 1) * HEAD_DIM] = (
                res[si, h][CHUNK:] + dot(d["attn"], vb[si, h])).astype(o_ref.dtype)
            st[si, h] = st[si, h] * jnp.exp(d["glast"]) + lax.dot_general(
                d["k_dec"], vb[si, h], (((0,), (0,)), ((), ())), preferred_element_type=F32)
    for si, h in chains:
        s_ref[si, h] = st[si, h]


def _split_dot_left(a, b):
    hi = b.astype(BF16)
    lo = (b - hi.astype(F32)).astype(BF16)
    return (jnp.dot(a, hi, preferred_element_type=F32)
            + jnp.dot(a, lo, preferred_element_type=F32))


def _gdn(qkv, small, a_log_f, dt_bias_f, a_log_b, dt_bias_b, lb):
    bsz, t, _ = qkv.shape
    nb = t // lb
    fwd = lambda i: i
    bwd = lambda i: nb - 1 - i

    def pad(vec, rev):
        off_a = 3 * H_C if rev else 2 * H_C
        return jnp.zeros((1, V7X_LANES), F32).at[0, off_a:off_a + H_C].set(vec)

    col = lambda j, blk: pl.BlockSpec((1, lb, W_C), lambda b, i: (b, blk(i), j))
    sm = lambda blk: pl.BlockSpec((1, lb, V7X_LANES), lambda b, i: (b, blk(i), 0))
    vec = pl.BlockSpec((1, V7X_LANES), lambda b, i: (0, 0))
    out = jax.ShapeDtypeStruct((bsz, t, W_C), BF16)
    return pl.pallas_call(
        functools.partial(_gdn_kernel, lb),
        grid=(bsz, nb),
        in_specs=[col(0, fwd), col(1, fwd), col(2, fwd), sm(fwd),
                  col(0, bwd), col(1, bwd), col(2, bwd), sm(bwd),
                  vec, vec, vec, vec],
        out_specs=[col(0, fwd), col(0, bwd)],
        out_shape=[out, out],
        scratch_shapes=[pltpu.VMEM((2, H_C, DK_C, DV_C), F32)],
        compiler_params=_cparams(("parallel", "arbitrary"), V7X_VMEM_LIMIT_BYTES),
        name="gdn",
    )(qkv, qkv, qkv, small, qkv, qkv, qkv, small,
      pad(a_log_f, False), pad(dt_bias_f, False), pad(a_log_b, True), pad(dt_bias_b, True))


def _gated_sum_kernel(x_ref, gn_ref, ma_ref, mb_ref, of_ref, ob_ref, gc_ref, og_ref,
                      wout_ref, wg0_ref, wg1_ref, wg2_ref, bg0_ref, bg1_ref, bg2_ref,
                      o_ref, xn_ref, mix_ref):
    j = pl.program_id(1)

    @pl.when(j == 0)
    def _():
        x = x_ref[...]
        xn = x * lax.rsqrt(jnp.mean(x * x, axis=-1, keepdims=True) + EPS) * gn_ref[...]
        xn_ref[...] = xn.astype(BF16)
        mix_ref[:, 0:W_A] = ma_ref[...]
        mix_ref[:, W_A:W_A + W_B] = mb_ref[...]
        for h in range(H_C):
            lanes = slice(h * HEAD_DIM, (h + 1) * HEAD_DIM)
            o = of_ref[:, lanes].astype(F32) + ob_ref[:, lanes].astype(F32)
            o = o * lax.rsqrt(jnp.mean(o * o, axis=-1, keepdims=True) + EPS) * og_ref[...]
            g = gc_ref[:, lanes].astype(F32)
            c0 = W_A + W_B + h * HEAD_DIM
            mix_ref[:, c0:c0 + HEAD_DIM] = (o * (g * jax.nn.sigmoid(g))).astype(BF16)

    xn = xn_ref[...]
    s = None
    for lo, hi, wg_ref, bg_ref in ((0, W_A, wg0_ref, bg0_ref),
                                   (W_A, W_A + W_B, wg1_ref, bg1_ref),
                                   (W_A + W_B, D_MODEL, wg2_ref, bg2_ref)):
        y = jnp.dot(mix_ref[:, lo:hi], wout_ref[lo:hi, :], preferred_element_type=F32)
        gate = jax.nn.sigmoid(jnp.dot(xn, wg_ref[...], preferred_element_type=F32) + bg_ref[...])
        s = gate * y if s is None else s + gate * y
    o_ref[...] = s.astype(o_ref.dtype)


def _gated_sum(x, norm_gain, mix_a, mix_b, o_f, o_b, main, o_gain, w_out, w_gate, b_gate, tm, tn):
    n, d = x.shape
    nj = d // tn
    cg = (COL_C + 3 * W_C) // W_C
    row = lambda w: pl.BlockSpec((tm, w), lambda i, j: (i, 0))
    wg = lambda br: pl.BlockSpec((d, tn), lambda i, j: (0, br * nj + j))
    bg = lambda br: pl.BlockSpec((1, tn), lambda i, j: (0, br * nj + j))
    return pl.pallas_call(
        _gated_sum_kernel,
        grid=(n // tm, nj),
        in_specs=[row(d),
                  pl.BlockSpec((1, d), lambda i, j: (0, 0)),
                  row(W_A), row(W_B), row(W_C), row(W_C),
                  pl.BlockSpec((tm, W_C), lambda i, j: (i, cg)),
                  pl.BlockSpec((1, HEAD_DIM), lambda i, j: (0, 0)),
                  pl.BlockSpec((d, tn), lambda i, j: (0, j)),
                  wg(0), wg(1), wg(2), bg(0), bg(1), bg(2)],
        out_specs=pl.BlockSpec((tm, tn), lambda i, j: (i, j)),
        out_shape=jax.ShapeDtypeStruct((n, d), BF16),
        scratch_shapes=[pltpu.VMEM((tm, d), BF16), pltpu.VMEM((tm, d), BF16)],
        compiler_params=_cparams(("parallel", "arbitrary"), V7X_VMEM_LIMIT_BYTES),
        name="gated_sum",
    )(x, norm_gain.reshape(1, d), mix_a, mix_b, o_f, o_b, main, o_gain.reshape(1, HEAD_DIM),
      w_out, w_gate, w_gate, w_gate, b_gate, b_gate, b_gate)


def _out_proj_kernel(x_ref, s_ref, w_ref, o_ref):
    o_ref[...] = x_ref[...] + jnp.dot(s_ref[...], w_ref[...], preferred_element_type=F32)


def _out_proj(x, s, w_o, tm):
    n, d = x.shape
    row = pl.BlockSpec((tm, d), lambda i: (i, 0))
    return pl.pallas_call(
        _out_proj_kernel,
        grid=(n // tm,),
        in_specs=[row, row, pl.BlockSpec((d, d), lambda i: (0, 0))],
        out_specs=row,
        out_shape=jax.ShapeDtypeStruct((n, d), F32),
        compiler_params=_cparams(("parallel",), V7X_VMEM_LIMIT_BYTES),
        name="out_proj",
    )(x, s, w_o)


def _router_kernel(x_ref, g_ref, wh_ref, wl_ref, h_ref, aff_ref):
    x = x_ref[...]
    h = x * lax.rsqrt(jnp.mean(x * x, axis=-1, keepdims=True) + EPS) * g_ref[...]
    h_ref[...] = h
    hh = h.astype(BF16)
    hl = (h - hh.astype(F32)).astype(BF16)
    logits = (jnp.dot(hh, wh_ref[...], preferred_element_type=F32)
              + jnp.dot(hl, wh_ref[...], preferred_element_type=F32)
              + jnp.dot(hh, wl_ref[...], preferred_element_type=F32))
    lane = lax.broadcasted_iota(I32, logits.shape, 1)
    valid = lane < N_EXPERTS
    logits = jnp.where(valid, logits, -jnp.inf)
    m = jnp.max(logits, axis=-1, keepdims=True)
    e = jnp.where(valid, jnp.exp(logits - m), 0.0)
    aff_ref[...] = e / jnp.sum(e, axis=-1, keepdims=True)


def _router(x, gain, w_router, tm):
    n, d = x.shape
    wpad = jnp.zeros((d, V7X_LANES), F32).at[:, :N_EXPERTS].set(w_router)
    w_hi = wpad.astype(BF16)
    w_lo = (wpad - w_hi.astype(F32)).astype(BF16)
    wspec = pl.BlockSpec((d, V7X_LANES), lambda i: (0, 0))
    return pl.pallas_call(
        _router_kernel,
        grid=(n // tm,),
        in_specs=[pl.BlockSpec((tm, d), lambda i: (i, 0)),
                  pl.BlockSpec((1, d), lambda i: (0, 0)), wspec, wspec],
        out_specs=[pl.BlockSpec((tm, d), lambda i: (i, 0)),
                   pl.BlockSpec((tm, V7X_LANES), lambda i: (i, 0))],
        out_shape=[jax.ShapeDtypeStruct((n, d), F32),
                   jax.ShapeDtypeStruct((n, V7X_LANES), F32)],
        compiler_params=_cparams(("parallel",), V7X_VMEM_LIMIT_BYTES),
        name="router",
    )(x, gain.reshape(1, d), w_hi, w_lo)


CUM_BLK = 256


def _select_kernel(cap, aff_ref, idx_ref, pos_ref, wsel_ref, starts_ref, cum_ref):
    t = aff_ref.shape[1]
    aff = aff_ref[0]

    def search(i, lo):
        cand = lo | jnp.left_shift(jnp.int32(1), 30 - i)
        cnt = jnp.sum((aff >= pltpu.bitcast(cand, F32)).astype(I32), axis=0, keepdims=True)
        return jnp.where(cnt >= cap, cand, lo)

    thr_bits = lax.fori_loop(0, 31, search, jnp.zeros((1, V7X_LANES), I32))
    thr = pltpu.bitcast(thr_bits, F32)
    gt = aff > thr
    eq = aff == thr
    need = (cap - jnp.sum(gt.astype(I32), axis=0, keepdims=True)).astype(F32)

    ri = lax.broadcasted_iota(I32, (CUM_BLK, CUM_BLK), 0)
    ci = lax.broadcasted_iota(I32, (CUM_BLK, CUM_BLK), 1)
    tri = (ri >= ci).astype(BF16)

    def cumsum_rows(mask_f32_of_block, store):
        off = jnp.zeros((1, V7X_LANES), F32)
        for blk in range(t // CUM_BLK):
            rows = slice(blk * CUM_BLK, (blk + 1) * CUM_BLK)
            c = jnp.dot(tri, mask_f32_of_block(rows).astype(BF16), preferred_element_type=F32) + off
            store(rows, c)
            off = c[CUM_BLK - 1:CUM_BLK, :]

    def store_cum(rows, c):
        cum_ref[rows, :] = c

    cumsum_rows(lambda rows: eq[rows].astype(F32), store_cum)
    sel = gt | (eq & (cum_ref[...] <= need))
    self32 = sel.astype(F32)
    cumsum_rows(lambda rows: self32[rows], store_cum)
    cum = cum_ref[...]
    pos_ref[0] = jnp.where(sel, cum.astype(I32) - 1, NEG_POS)
    wsel_ref[0] = jnp.where(sel, aff, 0.0)

    nstart = starts_ref.shape[1]
    tblk = t // (nstart - 1)
    starts_ref[0, 0:1, :] = jnp.zeros((1, V7X_LANES), I32)
    for tb in range(1, nstart):
        starts_ref[0, tb:tb + 1, :] = cum[tb * tblk - 1:tb * tblk, :].astype(I32)

    jj = lax.broadcasted_iota(I32, (1, cap), 1).astype(F32)
    ones = jnp.ones((8, CUM_BLK), BF16)
    for e in range(N_EXPERTS):
        acc = jnp.zeros((8, cap), F32)
        for blk in range(t // CUM_BLK):
            col = cum_ref[blk * CUM_BLK:(blk + 1) * CUM_BLK, e:e + 1]
            acc = acc + jnp.dot(ones, (col <= jj).astype(BF16), preferred_element_type=F32)
        idx_ref[0, e:e + 1, :] = acc[0:1, :].astype(I32)


def _select(aff, bsz, t, n_tok_blk):
    cap = CAPACITY_FACTOR * t // N_EXPERTS
    aff3 = aff.reshape(bsz, t, V7X_LANES)
    tl = pl.BlockSpec((1, t, V7X_LANES), lambda b: (b, 0, 0))
    return pl.pallas_call(
        functools.partial(_select_kernel, cap),
        grid=(bsz,),
        in_specs=[tl],
        out_specs=[pl.BlockSpec((1, N_EXPERTS, cap), lambda b: (b, 0, 0)), tl, tl,
                   pl.BlockSpec((1, n_tok_blk + 1, V7X_LANES), lambda b: (b, 0, 0))],
        out_shape=[jax.ShapeDtypeStruct((bsz, N_EXPERTS, cap), I32),
                   jax.ShapeDtypeStruct((bsz, t, V7X_LANES), I32),
                   jax.ShapeDtypeStruct((bsz, t, V7X_LANES), F32),
                   jax.ShapeDtypeStruct((bsz, n_tok_blk + 1, V7X_LANES), I32)],
        scratch_shapes=[pltpu.VMEM((t, V7X_LANES), F32)],
        compiler_params=_cparams(("parallel",), V7X_VMEM_LIMIT_BYTES),
        name="moe_select",
    )(aff3)


def _ffn_kernel(nf, nd, nbg, cap, cur_ref, nxt_ref, h_hbm, wg_ref, wu_ref, wd_ref, y_ref,
                xin, xb, hid, sem):
    s = pl.program_id(2)
    group = pl.program_id(0) * pl.num_programs(1) + pl.program_id(1)
    ngroups = pl.num_programs(0) * pl.num_programs(1)
    rows = nbg * cap
    per = rows // (nf + nd)
    tf = EXPERT_FF // nf

    def row_copy(idx_ref, j):
        return pltpu.make_async_copy(h_hbm.at[pl.ds(idx_ref[0, 0, j], 1), :],
                                     xin.at[pl.ds(j, 1), :], sem)

    def start_rows(idx_ref, base, n):
        def body(j, c):
            row_copy(idx_ref, base + j).start()
            return c
        lax.fori_loop(0, n, body, 0, unroll=8)

    def wait_rows(idx_ref, n):
        def body(j, c):
            row_copy(idx_ref, j).wait()
            return c
        lax.fori_loop(0, n, body, 0, unroll=8)

    @pl.when(s == 0)
    def _():
        @pl.when(group == 0)
        def _():
            start_rows(cur_ref, 0, rows)
        wait_rows(cur_ref, rows)
        xb[...] = xin[...].astype(BF16)

    def prefetch_rows():
        for j in range(per):
            row_copy(nxt_ref, s * per + j).start()

    @pl.when(s < nf)
    def _():
        prefetch_rows()
        x = xb[...]
        g = jnp.dot(x, wg_ref[0].astype(BF16), preferred_element_type=F32)
        u = jnp.dot(x, wu_ref[0].astype(BF16), preferred_element_type=F32)
        hid[s] = (g * jax.nn.sigmoid(g) * u).astype(BF16)

    @pl.when(s >= nf)
    def _():
        prefetch_rows()
        w = wd_ref[0].astype(BF16)
        acc = None
        for f in range(nf):
            part = jnp.dot(hid[f], w[f * tf:(f + 1) * tf, :], preferred_element_type=F32)
            acc = part if acc is None else acc + part
        for bi in range(nbg):
            y_ref[bi, 0] = acc[bi * cap:(bi + 1) * cap, :].astype(y_ref.dtype)

    @pl.when((group + 1 == ngroups) & (s + 1 == nf + nd))
    def _():
        wait_rows(nxt_ref, rows)


def _expert_ffn(idx, h, w_gate, w_up, w_down, layer, bsz, t, nf, nd):
    cap = idx.shape[-1]
    d = h.shape[1]
    nbg = 2 if bsz % 2 == 0 else 1
    nh = bsz // nbg
    rows = nbg * cap
    tf, tn = EXPERT_FF // nf, d // nd
    gidx = idx + (jnp.arange(bsz, dtype=I32) * t)[:, None, None]
    gidx = gidx.reshape(nh, nbg, N_EXPERTS, cap).transpose(2, 0, 1, 3).reshape(N_EXPERTS * nh, 1, rows)
    last = N_EXPERTS * nh - 1
    return pl.pallas_call(
        functools.partial(_ffn_kernel, nf, nd, nbg, cap),
        grid=(N_EXPERTS, nh, nf + nd),
        in_specs=[pl.BlockSpec((1, 1, rows), lambda e, hf, s: (e * nh + hf, 0, 0),
                               memory_space=pltpu.SMEM),
                  pl.BlockSpec((1, 1, rows), lambda e, hf, s: (jnp.minimum(e * nh + hf + 1, last), 0, 0),
                               memory_space=pltpu.SMEM),
                  pl.BlockSpec(memory_space=pl.ANY),
                  pl.BlockSpec((None, 1, d, tf), lambda e, hf, s: (layer, e, 0, jnp.minimum(s, nf - 1))),
                  pl.BlockSpec((None, 1, d, tf), lambda e, hf, s: (layer, e, 0, jnp.minimum(s, nf - 1))),
                  pl.BlockSpec((None, 1, EXPERT_FF, tn),
                               lambda e, hf, s: (layer, e, 0, jnp.maximum(s - nf, 0)))],
        out_specs=pl.BlockSpec((nbg, 1, cap, tn), lambda e, hf, s: (hf, e, 0, jnp.maximum(s - nf, 0))),
        out_shape=jax.ShapeDtypeStruct((bsz, N_EXPERTS, cap, d), BF16),
        scratch_shapes=[pltpu.VMEM((rows, d), F32), pltpu.VMEM((rows, d), BF16),
                        pltpu.VMEM((nf, rows, tf), BF16), pltpu.SemaphoreType.DMA],
        compiler_params=_cparams(("arbitrary", "arbitrary", "arbitrary"), V7X_VMEM_LIMIT_BYTES),
        name="moe_ffn",
    )(gidx, gidx, h, w_gate, w_up, w_down)


def _combine_kernel(cap, tblk, starts_ref, x_ref, pos_ref, w_ref, y_hbm, o_ref, kbuf, sems):
    b = pl.program_id(0)
    tb = pl.program_id(1)
    ntb = pl.num_programs(1)
    step = b * ntb + tb
    cur = step % 2
    o_ref[0] = x_ref[0]

    def firsts(bb, tt):
        return [(starts_ref[bb, tt, e] // 16) * 16 for e in range(N_EXPERTS)]

    def offsets(first, p):
        lows = [first[e] + p * ROW_WIN for e in range(N_EXPERTS)]
        return lows, [pl.multiple_of(jnp.minimum(lo, cap - ROW_WIN), 16) for lo in lows]

    def window_copy(bb, e, off, buf):
        return pltpu.make_async_copy(y_hbm.at[bb, e, pl.ds(off, ROW_WIN), :],
                                     kbuf.at[buf, pl.ds(e * ROW_WIN, ROW_WIN), :], sems.at[buf])

    def start_pass(bb, first, p, buf):
        _, offs = offsets(first, p)
        for e in range(N_EXPERTS):
            window_copy(bb, e, offs[e], buf).start()

    def finish_pass(first, p, buf):
        lows, offs = offsets(first, p)
        lane = lax.broadcasted_iota(I32, (tblk, V7X_LANES), 1)
        slot = lane % ROW_WIN
        head = lane < ROW_WIN
        tiles = []
        for g in range(N_EXPERTS * ROW_WIN // V7X_LANES):
            ea, eb = 2 * g, 2 * g + 1
            pe = jnp.where(head, pos_ref[0, :, ea:ea + 1], pos_ref[0, :, eb:eb + 1])
            we = jnp.where(head, w_ref[0, :, ea:ea + 1], w_ref[0, :, eb:eb + 1])
            off = jnp.where(head, offs[ea], offs[eb])
            low = jnp.where(head, lows[ea], lows[eb])
            hit = (pe == off + slot) & (pe >= low)
            tiles.append(jnp.where(hit, we, 0.0).astype(BF16))
        pmat = jnp.concatenate(tiles, axis=1)
        for e in range(N_EXPERTS):
            window_copy(b, e, offs[e], buf).wait()
        o_ref[0] += jnp.dot(pmat, kbuf[buf], preferred_element_type=F32)

    first = firsts(b, tb)
    npass = jnp.int32(1)
    for e in range(N_EXPERTS):
        npass = jnp.maximum(npass, (starts_ref[b, tb + 1, e] - first[e] + ROW_WIN - 1) // ROW_WIN)

    @pl.when(step == 0)
    def _():
        start_pass(b, first, 0, cur)

    @pl.when(step + 1 < pl.num_programs(0) * ntb)
    def _():
        wrap = tb + 1 == ntb
        nb = jnp.where(wrap, b + 1, b)
        nt = jnp.where(wrap, 0, tb + 1)
        start_pass(nb, firsts(nb, nt), 0, 1 - cur)

    finish_pass(first, 0, cur)

    def extra_pass(p, carry):
        start_pass(b, first, p, cur)
        finish_pass(first, p, cur)
        return carry

    lax.fori_loop(1, npass, extra_pass, 0)


def _combine(x, pos, wsel, starts, y, tblk):
    bsz, t, d = x.shape
    cap = y.shape[2]
    return pl.pallas_call(
        functools.partial(_combine_kernel, cap, tblk),
        grid=(bsz, t // tblk),
        in_specs=[pl.BlockSpec(memory_space=pltpu.SMEM),
                  pl.BlockSpec((1, tblk, d), lambda b, i: (b, i, 0)),
                  pl.BlockSpec((1, tblk, V7X_LANES), lambda b, i: (b, i, 0)),
                  pl.BlockSpec((1, tblk, V7X_LANES), lambda b, i: (b, i, 0)),
                  pl.BlockSpec(memory_space=pl.ANY)],
        out_specs=pl.BlockSpec((1, tblk, d), lambda b, i: (b, i, 0)),
        out_shape=jax.ShapeDtypeStruct((bsz, t, d), F32),
        scratch_shapes=[pltpu.VMEM((2, N_EXPERTS * ROW_WIN, d), BF16), pltpu.SemaphoreType.DMA((2,))],
        compiler_params=_cparams(("arbitrary", "arbitrary")),
        name="moe_combine",
    )(starts, x, pos, wsel, y)


def _tiles(n, t):
    pick = lambda want, total: want if total % want == 0 else total
    return dict(
        tm_proj=pick(1024, n), tn_proj=768,
        tq=pick(1024, t), tk=pick(512, t), lb=pick(512, t),
        tm_gsum=pick(1024, n), tn_gsum=256, tm_oproj=pick(512, n),
        nf_ffn=4, nd_ffn=4,
        tblk=pick(256, t),
    )


def kernel(x, positions, norm_mix, w_in, conv_a, q_norm, k_norm, lambda_q1, lambda_k1, lambda_q2, lambda_k2, subln, conv_c, a_log_f, a_log_b, dt_bias_f, dt_bias_b, o_norm, w_out_a, w_out_b, w_out_c, w_gate, b_gate, w_o, norm_ffn, w_router, w_e_gate, w_e_up, w_e_down):
    bsz, t, d = x.shape
    n = bsz * t
    depth = w_in.shape[0]
    ts = _tiles(n, t)
    tables = _rope_tables(positions)
    pad_small = jnp.zeros((d, V7X_LANES - N_SMALL), F32)
    tile2 = lambda g: jnp.concatenate([g, g]).reshape(1, V7X_LANES)
    pad64 = lambda vv: jnp.concatenate([vv, jnp.zeros_like(vv)])
    w_in_bf = w_in.astype(BF16)

    xf = x.reshape(n, d)
    for l in range(depth):
        lam_init = 0.8 - 0.6 * math.exp(-0.3 * l)
        w_small = jnp.concatenate([w_in[l, :, N_MAIN:], pad_small], axis=1)
        main, small = _norm_matmul(xf, norm_mix[l], w_in_bf, l, N_MAIN, w_small,
                                   ts["tm_proj"], ts["tn_proj"])
        main3 = main.reshape(bsz, t, N_MAIN)
        small3 = small.reshape(bsz, t, V7X_LANES)

        mix_a = _mixer_a(main3, conv_a[l])
        lamv = jnp.stack([pad64(lambda_q1[l]), pad64(lambda_k1[l]),
                          pad64(lambda_q2[l]), pad64(lambda_k2[l])])
        mix_b = _diff_attention(main3, tables, tile2(q_norm[l]), tile2(k_norm[l]), lamv,
                                subln[l].reshape(1, V7X_LANES), lam_init, ts["tq"], ts["tk"])
        qkv_c = _gdn_prep(main3, conv_c[l])
        o_f, o_b = _gdn(qkv_c, small3, a_log_f[l], dt_bias_f[l], a_log_b[l], dt_bias_b[l], ts["lb"])

        w_out = jnp.concatenate([w_out_a[l], w_out_b[l], w_out_c[l]], axis=0).astype(BF16)
        gsum = _gated_sum(xf, norm_mix[l], mix_a.reshape(n, W_A), mix_b.reshape(n, W_B),
                          o_f.reshape(n, W_C), o_b.reshape(n, W_C), main, o_norm[l],
                          w_out, w_gate[l].astype(BF16), b_gate[l].reshape(1, N_BRANCH * d),
                          ts["tm_gsum"], ts["tn_gsum"])
        x1 = _out_proj(xf, gsum, w_o[l].astype(BF16), ts["tm_oproj"])
        h, aff = _router(x1, norm_ffn[l], w_router[l], ts["tm_oproj"])
        idx, pos, wsel, starts = _select(aff, bsz, t, t // ts["tblk"])
        y = _expert_ffn(idx, h, w_e_gate, w_e_up, w_e_down, l, bsz, t, ts["nf_ffn"], ts["nd_ffn"])
        xf = _combine(x1.reshape(bsz, t, d), pos, wsel, starts, y, ts["tblk"]).reshape(n, d)
    return xf.reshape(bsz, t, d)
```

```python
import functools
import math

import jax
import jax.numpy as jnp
from jax import lax
from jax.experimental import pallas as pl
from jax.experimental.pallas import tpu as pltpu

F32 = jnp.float32
BF16 = jnp.bfloat16
I32 = jnp.int32

D_MODEL = 2048
HEAD_DIM = 128
W_A = D_MODEL // 4
DH_B = 64
DV_B = 2 * DH_B
H_B = (3 * D_MODEL // 8) // DV_B
W_B = H_B * DV_B
ROT_DIM = DH_B // 4
ROPE_THETA = 500000.0
DK_C = HEAD_DIM
DV_C = HEAD_DIM
H_C = (3 * D_MODEL // 8) // DV_C
W_C = H_C * DV_C
CHUNK = 64
N_BRANCH = 3
N_EXPERTS = 16
CAPACITY_FACTOR = 2
EXPERT_FF = D_MODEL // 2
EPS = 1e-6

N_MAIN = 3 * W_A + 3 * W_B + 4 * W_C
N_SMALL = 4 * H_C
COL_A = 0
COL_B = 3 * W_A
COL_C = COL_B + 3 * W_B

V7X_LANES = 128
BF16_SUBLANES = 16
V7X_VMEM_LIMIT_BYTES = 56 * 1024 * 1024
ROW_WIN = 64
NEG_POS = -(1 << 20)


def _cparams(sem, vmem=None):
    return pltpu.CompilerParams(dimension_semantics=sem, vmem_limit_bytes=vmem)


def _split_dot(a, b):
    hi = a.astype(BF16)
    lo = (a - hi.astype(F32)).astype(BF16)
    return (jnp.dot(hi, b, preferred_element_type=F32)
            + jnp.dot(lo, b, preferred_element_type=F32))


def _norm_matmul_kernel(x_ref, g_ref, w_ref, ws_ref, o_ref, os_ref, xn_ref):
    @pl.when(pl.program_id(1) == 0)
    def _():
        x = x_ref[...]
        xn = x * lax.rsqrt(jnp.mean(x * x, axis=-1, keepdims=True) + EPS) * g_ref[...]
        xn_ref[...] = xn.astype(BF16)
        os_ref[...] = jnp.dot(xn_ref[...], ws_ref[...].astype(BF16), preferred_element_type=F32)

    o_ref[...] = jnp.dot(xn_ref[...], w_ref[...], preferred_element_type=F32).astype(o_ref.dtype)


def _norm_matmul(x, gain, w, layer, nout, w_small, tm, tn):
    n, d = x.shape
    return pl.pallas_call(
        _norm_matmul_kernel,
        grid=(n // tm, nout // tn),
        in_specs=[pl.BlockSpec((tm, d), lambda i, j: (i, 0)),
                  pl.BlockSpec((1, d), lambda i, j: (0, 0)),
                  pl.BlockSpec((None, d, tn), lambda i, j: (layer, 0, j)),
                  pl.BlockSpec((d, V7X_LANES), lambda i, j: (0, 0))],
        out_specs=[pl.BlockSpec((tm, tn), lambda i, j: (i, j)),
                   pl.BlockSpec((tm, V7X_LANES), lambda i, j: (i, 0))],
        out_shape=[jax.ShapeDtypeStruct((n, nout), BF16),
                   jax.ShapeDtypeStruct((n, V7X_LANES), F32)],
        scratch_shapes=[pltpu.VMEM((tm, d), BF16)],
        compiler_params=_cparams(("parallel", "arbitrary"), V7X_VMEM_LIMIT_BYTES),
        name="norm_matmul",
    )(x, gain.reshape(1, d), w, w_small)


def _conv3(x, w):
    t = x.shape[0]
    row = lax.broadcasted_iota(I32, x.shape, 0)
    prev = jnp.where(row == 0, 0.0, pltpu.roll(x, 1, 0))
    nxt = jnp.where(row == t - 1, 0.0, pltpu.roll(x, t - 1, 0))
    return prev * w[0:1, :] + x * w[1:2, :] + nxt * w[2:3, :]


def _mixer_a_kernel(b_ref, c_ref, v_ref, w_ref, o_ref):
    cv = c_ref[0].astype(F32) * v_ref[0].astype(F32)
    o_ref[0] = (b_ref[0].astype(F32) * _conv3(cv, w_ref[...])).astype(o_ref.dtype)


def _mixer_a(main, conv_w):
    bsz, t, _ = main.shape
    nblk = W_A // V7X_LANES
    c0 = COL_A // V7X_LANES
    spec = lambda off: pl.BlockSpec((1, t, V7X_LANES), lambda b, c: (b, 0, c0 + off + c))
    return pl.pallas_call(
        _mixer_a_kernel,
        grid=(bsz, nblk),
        in_specs=[spec(0), spec(nblk), spec(2 * nblk),
                  pl.BlockSpec((3, V7X_LANES), lambda b, c: (0, c))],
        out_specs=pl.BlockSpec((1, t, V7X_LANES), lambda b, c: (b, 0, c)),
        out_shape=jax.ShapeDtypeStruct((bsz, t, W_A), BF16),
        compiler_params=_cparams(("parallel", "parallel")),
        name="mixer_a",
    )(main, main, main, conv_w)


def _gdn_prep_kernel(x_ref, w_ref, o_ref):
    c = pl.program_id(1)
    y = _conv3(x_ref[0].astype(F32), w_ref[...])
    y = y * jax.nn.sigmoid(y)
    inv = lax.rsqrt(jnp.sum(y * y, axis=-1, keepdims=True) + EPS)
    scale = jnp.where(c < H_C, inv * (DK_C ** -0.5), jnp.where(c < 2 * H_C, inv, 1.0))
    o_ref[0] = (y * scale).astype(o_ref.dtype)


def _gdn_prep(main, conv_w):
    bsz, t, _ = main.shape
    nblk = 3 * W_C // V7X_LANES
    c0 = COL_C // V7X_LANES
    return pl.pallas_call(
        _gdn_prep_kernel,
        grid=(bsz, nblk),
        in_specs=[pl.BlockSpec((1, t, V7X_LANES), lambda b, c: (b, 0, c0 + c)),
                  pl.BlockSpec((3, V7X_LANES), lambda b, c: (0, c))],
        out_specs=pl.BlockSpec((1, t, V7X_LANES), lambda b, c: (b, 0, c)),
        out_shape=jax.ShapeDtypeStruct((bsz, t, 3 * W_C), BF16),
        compiler_params=_cparams(("parallel", "parallel")),
        name="gdn_prep",
    )(main, conv_w)


def _rope_kernel(pos_ref, c_ref, s1_ref, s2_ref):
    pos = pos_ref[0].astype(F32)
    lane = lax.broadcasted_iota(I32, (1, V7X_LANES), 1)
    l64 = lane % DH_B
    half = ROT_DIM // 2
    fidx = jnp.where(l64 < half, l64, l64 - half).astype(F32)
    inv_freq = jnp.exp(fidx * (-math.log(ROPE_THETA) * 2.0 / ROT_DIM))
    ang = pos * inv_freq
    cos, sin = jnp.cos(ang), jnp.sin(ang)
    first = l64 < half
    second = (l64 >= half) & (l64 < ROT_DIM)
    c_ref[0] = jnp.where(first | second, cos, 1.0)
    s1_ref[0] = jnp.where(first, -sin, 0.0)
    s2_ref[0] = jnp.where(second, sin, 0.0)


def _rope_tables(positions):
    bsz, t = positions.shape
    shp = jax.ShapeDtypeStruct((bsz, t, V7X_LANES), F32)
    spec = pl.BlockSpec((1, t, V7X_LANES), lambda b: (b, 0, 0))
    return pl.pallas_call(
        _rope_kernel,
        grid=(bsz,),
        in_specs=[pl.BlockSpec((1, t, 1), lambda b: (b, 0, 0))],
        out_specs=[spec, spec, spec],
        out_shape=[shp, shp, shp],
        compiler_params=_cparams(("parallel",)),
        name="rope_tables",
    )(positions.reshape(bsz, t, 1))


def _qk_prep(x, gain, cos, s1, s2):
    r = lax.broadcasted_iota(I32, (V7X_LANES, V7X_LANES), 0) // DH_B
    c = lax.broadcasted_iota(I32, (V7X_LANES, V7X_LANES), 1) // DH_B
    blockdiag = (r == c).astype(BF16)
    ssq = _split_dot(x * x, blockdiag)
    xn = x * lax.rsqrt(ssq * (1.0 / DH_B) + EPS) * gain
    half = ROT_DIM // 2
    return xn * cos + pltpu.roll(xn, V7X_LANES - half, 1) * s1 + pltpu.roll(xn, half, 1) * s2


def _attn_kernel(lam_init, tq, tk, q_ref, k_ref, v_ref, c_ref, s1_ref, s2_ref,
                 qg_ref, kg_ref, lamv_ref, sub_ref, o_ref, kp_ref, vx_ref, acc_ref):
    qi = pl.program_id(2)
    t = kp_ref.shape[0]

    @pl.when(qi == 0)
    def _():
        kp_ref[...] = _qk_prep(k_ref[0].astype(F32), kg_ref[...],
                               c_ref[0], s1_ref[0], s2_ref[0]).astype(BF16)
        vx_ref[:DV_B, :] = v_ref[0].astype(F32).T.astype(BF16)
        vx_ref[DV_B:, :] = jnp.ones((vx_ref.shape[0] - DV_B, t), BF16)

    rows = pl.ds(pl.multiple_of(qi * tq, tq), tq)
    q = _qk_prep(q_ref[0].astype(F32), qg_ref[...], c_ref[0, rows, :], s1_ref[0, rows, :],
                 s2_ref[0, rows, :]) * (DH_B ** -0.5 * math.log2(math.e))
    lane = lax.broadcasted_iota(I32, (tq, V7X_LANES), 1)
    qt = [jnp.where(keep, q, 0.0).T.astype(BF16) for keep in (lane < DH_B, lane >= DH_B)]

    def scores(j):
        kj = kp_ref[j * tk:(j + 1) * tk, :]
        return [jnp.dot(kj, qc, preferred_element_type=F32) for qc in qt]

    acc_ref[...] = jnp.zeros_like(acc_ref)
    m = [jnp.full((1, tq), -jnp.inf, F32)] * 2
    nkb = t // tk
    ahead = 2
    pending = [scores(j) for j in range(min(ahead, nkb))]
    for j in range(nkb):
        cur = pending.pop(0)
        if j + ahead < nkb:
            pending.append(scores(j + ahead))
        vj = vx_ref[:, j * tk:(j + 1) * tk]
        for c in range(2):
            m_new = jnp.maximum(m[c], jnp.max(cur[c], axis=0, keepdims=True))
            e = jnp.exp2(cur[c] - m_new).astype(BF16)
            acc_ref[c] = acc_ref[c] * jnp.exp2(m[c] - m_new) + jnp.dot(vj, e, preferred_element_type=F32)
            m[c] = m_new
    pv = [acc_ref[c, :DV_B, :] for c in range(2)]
    l = [acc_ref[c, DV_B:DV_B + 1, :] for c in range(2)]
    lv = lamv_ref[...]
    lam = (jnp.exp(jnp.sum(lv[0:1] * lv[1:2], axis=-1, keepdims=True))
           - jnp.exp(jnp.sum(lv[2:3] * lv[3:4], axis=-1, keepdims=True)) + lam_init)
    o = (pv[0] / l[0] - lam * (pv[1] / l[1])).T
    o = o * lax.rsqrt(jnp.mean(o * o, axis=-1, keepdims=True) + EPS) * sub_ref[...]
    o_ref[0] = (o * (1.0 - lam_init)).astype(o_ref.dtype)


def _diff_attention(main, tables, q_gain, k_gain, lamv, subln, lam_init, tq, tk):
    bsz, t, _ = main.shape
    cq = COL_B // V7X_LANES
    ck = cq + H_B
    cv = ck + H_B
    tab = pl.BlockSpec((1, t, V7X_LANES), lambda b, h, i: (b, 0, 0))
    vec = pl.BlockSpec((1, V7X_LANES), lambda b, h, i: (0, 0))
    return pl.pallas_call(
        functools.partial(_attn_kernel, lam_init, tq, tk),
        grid=(bsz, H_B, t // tq),
        in_specs=[pl.BlockSpec((1, tq, V7X_LANES), lambda b, h, i: (b, i, cq + h)),
                  pl.BlockSpec((1, t, V7X_LANES), lambda b, h, i: (b, 0, ck + h)),
                  pl.BlockSpec((1, t, V7X_LANES), lambda b, h, i: (b, 0, cv + h)),
                  tab, tab, tab, vec, vec,
                  pl.BlockSpec((4, V7X_LANES), lambda b, h, i: (0, 0)),
                  vec],
        out_specs=pl.BlockSpec((1, tq, V7X_LANES), lambda b, h, i: (b, i, h)),
        out_shape=jax.ShapeDtypeStruct((bsz, t, W_B), BF16),
        scratch_shapes=[pltpu.VMEM((t, V7X_LANES), BF16),
                        pltpu.VMEM((DV_B + BF16_SUBLANES, t), BF16),
                        pltpu.VMEM((2, DV_B + BF16_SUBLANES, tq), F32)],
        compiler_params=_cparams(("parallel", "parallel", "arbitrary"), V7X_VMEM_LIMIT_BYTES),
        name="diff_attention",
    )(main, main, main, *tables, q_gain, k_gain, lamv, subln)


def _gdn_kernel(lb, qf_ref, kf_ref, vf_ref, smf_ref, qb_ref, kb_ref, vb_ref, smb_ref,
                alf_ref, dtf_ref, alb_ref, dtb_ref, of_ref, ob_ref, s_ref):
    @pl.when(pl.program_id(1) == 0)
    def _():
        s_ref[...] = jnp.zeros_like(s_ref)

    nchunk = lb // CHUNK
    ri = lax.broadcasted_iota(I32, (CHUNK, CHUNK), 0)
    ci = lax.broadcasted_iota(I32, (CHUNK, CHUNK), 1)
    eye = (ri == ci).astype(F32)
    sides = []
    for rev, refs in ((False, (qf_ref, kf_ref, vf_ref, smf_ref, alf_ref, dtf_ref, of_ref)),
                      (True, (qb_ref, kb_ref, vb_ref, smb_ref, alb_ref, dtb_ref, ob_ref))):
        incl = (ri <= ci) if rev else (ri >= ci)
        sides.append(dict(
            q=refs[0], k=refs[1], v=refs[2], sm=refs[3], alog=refs[4], dtb=refs[5], o=refs[6],
            incl=incl, strict=(ri < ci) if rev else (ri > ci), cum_mat=incl.astype(BF16),
            last=0 if rev else CHUNK - 1,
            off_b=H_C if rev else 0,
            off_a=3 * H_C if rev else 2 * H_C,
            order=list(range(nchunk - 1, -1, -1) if rev else range(nchunk))))
    units = [(si, sides[si]["order"][k], h) for k in range(nchunk) for si in range(2) for h in range(H_C)]
    nt = (((1,), (1,)), ((), ()))
    dot = functools.partial(jnp.dot, preferred_element_type=F32)

    gates = {}
    for si, sd in enumerate(sides):
        for c in sd["order"]:
            rows = slice(c * CHUNK, (c + 1) * CHUNK)
            sm = sd["sm"][0, rows, :]
            g_all = -jnp.exp(sd["alog"][...]) * jax.nn.softplus(sm + sd["dtb"][...])
            gc_all = _split_dot_left(sd["cum_mat"], g_all)
            gates[si, c] = (jax.nn.sigmoid(sm), gc_all, gc_all.T)

    pre = {}
    for si, c, h in units:
        sd = sides[si]
        rows = slice(c * CHUNK, (c + 1) * CHUNK)
        lanes = slice(h * HEAD_DIM, (h + 1) * HEAD_DIM)
        beta_all, gc_all, gct_all = gates[si, c]
        q = sd["q"][0, rows, lanes].astype(F32)
        k = sd["k"][0, rows, lanes]
        kf = k.astype(F32)
        beta = beta_all[:, sd["off_b"] + h:sd["off_b"] + h + 1]
        gcol = gc_all[:, sd["off_a"] + h:sd["off_a"] + h + 1]
        grow = gct_all[sd["off_a"] + h:sd["off_a"] + h + 1, :]
        glast = gcol[sd["last"]:sd["last"] + 1, :]
        decay = jnp.where(sd["incl"], jnp.exp(jnp.where(sd["incl"], gcol - grow, 0.0)), 0.0)
        kb = kf * beta
        eg = jnp.exp(gcol)
        rhs = jnp.concatenate([sd["v"][0, rows, lanes].astype(F32) * beta, kb * eg], axis=-1).astype(BF16)
        pre[si, c, h] = dict(k=k, kb=kb, decay=decay, rhs=rhs, glast=glast,
                             q=q, q_dec=q * eg, k_dec=(kf * jnp.exp(glast - gcol)).astype(BF16))

    for u in units:
        d = pre[u]
        res = lax.dot_general(jnp.concatenate([d["kb"], d["q"]], axis=0).astype(BF16), d["k"], nt,
                              preferred_element_type=F32)
        d["low"] = jnp.where(sides[u[0]]["strict"], res[:CHUNK] * d["decay"], 0.0)
        d["attn"] = (res[CHUNK:] * d["decay"]).astype(BF16)
    for u in units:
        d = pre[u]
        lb16 = d["low"].astype(BF16)
        d["n"] = dot(lb16, lb16)
        d["p"] = eye - d["low"]
    for r in range(1, 6):
        for u in units:
            d = pre[u]
            nb16 = d["n"].astype(BF16)
            if r < 5:
                res = dot(jnp.concatenate([d["n"], d["p"]], axis=0).astype(BF16), nb16)
                d["n"] = res[:CHUNK]
                d["p"] = d["p"] + res[CHUNK:]
            else:
                d["p"] = d["p"] + dot(d["p"].astype(BF16), nb16)
    for u in units:
        d = pre[u]
        d["sol"] = dot(d["p"].astype(BF16), d["rhs"])

    chains = [(si, h) for si in range(2) for h in range(H_C)]
    st = {ch: s_ref[ch[0], ch[1]] for ch in chains}
    for kk in range(nchunk):
        res, vb = {}, {}
        for si, h in chains:
            d = pre[si, sides[si]["order"][kk], h]
            lhs = jnp.concatenate([d["sol"][:, DV_C:], d["q_dec"]], axis=0).astype(BF16)
            res[si, h] = dot(lhs, st[si, h].astype(BF16))
        for si, h in chains:
            d = pre[si, sides[si]["order"][kk], h]
            vb[si, h] = (d["sol"][:, :DV_C] - res[si, h][:CHUNK]).astype(BF16)
        for si, h in chains:
            c = sides[si]["order"][kk]
            d = pre[si, c, h]
            o_ref = sides[si]["o"]
            o_ref[0, c * CHUNK:(c + 1) * CHUNK, h * HEAD_DIM:(h + 1) * HEAD_DIM] = (
                res[si, h][CHUNK:] + dot(d["attn"], vb[si, h])).astype(o_ref.dtype)
            st[si, h] = st[si, h] * jnp.exp(d["glast"]) + lax.dot_general(
                d["k_dec"], vb[si, h], (((0,), (0,)), ((), ())), preferred_element_type=F32)
    for si, h in chains:
        s_ref[si, h] = st[si, h]


def _split_dot_left(a, b):
    hi = b.astype(BF16)
    lo = (b - hi.astype(F32)).astype(BF16)
    return (jnp.dot(a, hi, preferred_element_type=F32)
            + jnp.dot(a, lo, preferred_element_type=F32))


def _gdn(qkv, small, a_log_f, dt_bias_f, a_log_b, dt_bias_b, lb):
    bsz, t, _ = qkv.shape
    nb = t // lb
    fwd = lambda i: i
    bwd = lambda i: nb - 1 - i

    def pad(vec, rev):
        off_a = 3 * H_C if rev else 2 * H_C
        return jnp.zeros((1, V7X_LANES), F32).at[0, off_a:off_a + H_C].set(vec)

    col = lambda j, blk: pl.BlockSpec((1, lb, W_C), lambda b, i: (b, blk(i), j))
    sm = lambda blk: pl.BlockSpec((1, lb, V7X_LANES), lambda b, i: (b, blk(i), 0))
    vec = pl.BlockSpec((1, V7X_LANES), lambda b, i: (0, 0))
    out = jax.ShapeDtypeStruct((bsz, t, W_C), BF16)
    return pl.pallas_call(
        functools.partial(_gdn_kernel, lb),
        grid=(bsz, nb),
        in_specs=[col(0, fwd), col(1, fwd), col(2, fwd), sm(fwd),
                  col(0, bwd), col(1, bwd), col(2, bwd), sm(bwd),
                  vec, vec, vec, vec],
        out_specs=[col(0, fwd), col(0, bwd)],
        out_shape=[out, out],
        scratch_shapes=[pltpu.VMEM((2, H_C, DK_C, DV_C), F32)],
        compiler_params=_cparams(("parallel", "arbitrary"), V7X_VMEM_LIMIT_BYTES),
        name="gdn",
    )(qkv, qkv, qkv, small, qkv, qkv, qkv, small,
      pad(a_log_f, False), pad(dt_bias_f, False), pad(a_log_b, True), pad(dt_bias_b, True))


def _gated_sum_kernel(x_ref, gn_ref, ma_ref, mb_ref, of_ref, ob_ref, gc_ref, og_ref,
                      wout_ref, wg0_ref, wg1_ref, wg2_ref, bg0_ref, bg1_ref, bg2_ref,
                      o_ref, xn_ref, mix_ref):
    j = pl.program_id(1)

    @pl.when(j == 0)
    def _():
        x = x_ref[...]
        xn = x * lax.rsqrt(jnp.mean(x * x, axis=-1, keepdims=True) + EPS) * gn_ref[...]
        xn_ref[...] = xn.astype(BF16)
        mix_ref[:, 0:W_A] = ma_ref[...]
        mix_ref[:, W_A:W_A + W_B] = mb_ref[...]
        for h in range(H_C):
            lanes = slice(h * HEAD_DIM, (h + 1) * HEAD_DIM)
            o = of_ref[:, lanes].astype(F32) + ob_ref[:, lanes].astype(F32)
            o = o * lax.rsqrt(jnp.mean(o * o, axis=-1, keepdims=True) + EPS) * og_ref[...]
            g = gc_ref[:, lanes].astype(F32)
            c0 = W_A + W_B + h * HEAD_DIM
            mix_ref[:, c0:c0 + HEAD_DIM] = (o * (g * jax.nn.sigmoid(g))).astype(BF16)

    xn = xn_ref[...]
    s = None
    for lo, hi, wg_ref, bg_ref in ((0, W_A, wg0_ref, bg0_ref),
                                   (W_A, W_A + W_B, wg1_ref, bg1_ref),
                                   (W_A + W_B, D_MODEL, wg2_ref, bg2_ref)):
        y = jnp.dot(mix_ref[:, lo:hi], wout_ref[lo:hi, :], preferred_element_type=F32)
        gate = jax.nn.sigmoid(jnp.dot(xn, wg_ref[...], preferred_element_type=F32) + bg_ref[...])
        s = gate * y if s is None else s + gate * y
    o_ref[...] = s.astype(o_ref.dtype)


def _gated_sum(x, norm_gain, mix_a, mix_b, o_f, o_b, main, o_gain, w_out, w_gate, b_gate, tm, tn):
    n, d = x.shape
    nj = d // tn
    cg = (COL_C + 3 * W_C) // W_C
    row = lambda w: pl.BlockSpec((tm, w), lambda i, j: (i, 0))
    wg = lambda br: pl.BlockSpec((d, tn), lambda i, j: (0, br * nj + j))
    bg = lambda br: pl.BlockSpec((1, tn), lambda i, j: (0, br * nj + j))
    return pl.pallas_call(
        _gated_sum_kernel,
        grid=(n // tm, nj),
        in_specs=[row(d),
                  pl.BlockSpec((1, d), lambda i, j: (0, 0)),
                  row(W_A), row(W_B), row(W_C), row(W_C),
                  pl.BlockSpec((tm, W_C), lambda i, j: (i, cg)),
                  pl.BlockSpec((1, HEAD_DIM), lambda i, j: (0, 0)),
                  pl.BlockSpec((d, tn), lambda i, j: (0, j)),
                  wg(0), wg(1), wg(2), bg(0), bg(1), bg(2)],
        out_specs=pl.BlockSpec((tm, tn), lambda i, j: (i, j)),
        out_shape=jax.ShapeDtypeStruct((n, d), BF16),
        scratch_shapes=[pltpu.VMEM((tm, d), BF16), pltpu.VMEM((tm, d), BF16)],
        compiler_params=_cparams(("parallel", "arbitrary"), V7X_VMEM_LIMIT_BYTES),
        name="gated_sum",
    )(x, norm_gain.reshape(1, d), mix_a, mix_b, o_f, o_b, main, o_gain.reshape(1, HEAD_DIM),
      w_out, w_gate, w_gate, w_gate, b_gate, b_gate, b_gate)


def _out_proj_kernel(x_ref, s_ref, w_ref, o_ref):
    o_ref[...] = x_ref[...] + jnp.dot(s_ref[...], w_ref[...], preferred_element_type=F32)


def _out_proj(x, s, w_o, tm):
    n, d = x.shape
    row = pl.BlockSpec((tm, d), lambda i: (i, 0))
    return pl.pallas_call(
        _out_proj_kernel,
        grid=(n // tm,),
        in_specs=[row, row, pl.BlockSpec((d, d), lambda i: (0, 0))],
        out_specs=row,
        out_shape=jax.ShapeDtypeStruct((n, d), F32),
        compiler_params=_cparams(("parallel",), V7X_VMEM_LIMIT_BYTES),
        name="out_proj",
    )(x, s, w_o)


def _router_kernel(x_ref, g_ref, wh_ref, wl_ref, h_ref, aff_ref):
    x = x_ref[...]
    h = x * lax.rsqrt(jnp.mean(x * x, axis=-1, keepdims=True) + EPS) * g_ref[...]
    h_ref[...] = h
    hh = h.astype(BF16)
    hl = (h - hh.astype(F32)).astype(BF16)
    logits = (jnp.dot(hh, wh_ref[...], preferred_element_type=F32)
              + jnp.dot(hl, wh_ref[...], preferred_element_type=F32)
              + jnp.dot(hh, wl_ref[...], preferred_element_type=F32))
    lane = lax.broadcasted_iota(I32, logits.shape, 1)
    valid = lane < N_EXPERTS
    logits = jnp.where(valid, logits, -jnp.inf)
    m = jnp.max(logits, axis=-1, keepdims=True)
    e = jnp.where(valid, jnp.exp(logits - m), 0.0)
    aff_ref[...] = e / jnp.sum(e, axis=-1, keepdims=True)


def _router(x, gain, w_router, tm):
    n, d = x.shape
    wpad = jnp.zeros((d, V7X_LANES), F32).at[:, :N_EXPERTS].set(w_router)
    w_hi = wpad.astype(BF16)
    w_lo = (wpad - w_hi.astype(F32)).astype(BF16)
    wspec = pl.BlockSpec((d, V7X_LANES), lambda i: (0, 0))
    return pl.pallas_call(
        _router_kernel,
        grid=(n // tm,),
        in_specs=[pl.BlockSpec((tm, d), lambda i: (i, 0)),
                  pl.BlockSpec((1, d), lambda i: (0, 0)), wspec, wspec],
        out_specs=[pl.BlockSpec((tm, d), lambda i: (i, 0)),
                   pl.BlockSpec((tm, V7X_LANES), lambda i: (i, 0))],
        out_shape=[jax.ShapeDtypeStruct((n, d), F32),
                   jax.ShapeDtypeStruct((n, V7X_LANES), F32)],
        compiler_params=_cparams(("parallel",), V7X_VMEM_LIMIT_BYTES),
        name="router",
    )(x, gain.reshape(1, d), w_hi, w_lo)


CUM_BLK = 256


def _select_kernel(cap, aff_ref, idx_ref, pos_ref, wsel_ref, starts_ref, cum_ref):
    t = aff_ref.shape[1]
    aff = aff_ref[0]

    def search(i, lo):
        cand = lo | jnp.left_shift(jnp.int32(1), 30 - i)
        cnt = jnp.sum((aff >= pltpu.bitcast(cand, F32)).astype(I32), axis=0, keepdims=True)
        return jnp.where(cnt >= cap, cand, lo)

    thr_bits = lax.fori_loop(0, 31, search, jnp.zeros((1, V7X_LANES), I32))
    thr = pltpu.bitcast(thr_bits, F32)
    gt = aff > thr
    eq = aff == thr
    need = (cap - jnp.sum(gt.astype(I32), axis=0, keepdims=True)).astype(F32)

    ri = lax.broadcasted_iota(I32, (CUM_BLK, CUM_BLK), 0)
    ci = lax.broadcasted_iota(I32, (CUM_BLK, CUM_BLK), 1)
    tri = (ri >= ci).astype(BF16)

    def cumsum_rows(mask_f32_of_block, store):
        off = jnp.zeros((1, V7X_LANES), F32)
        for blk in range(t // CUM_BLK):
            rows = slice(blk * CUM_BLK, (blk + 1) * CUM_BLK)
            c = jnp.dot(tri, mask_f32_of_block(rows).astype(BF16), preferred_element_type=F32) + off
            store(rows, c)
            off = c[CUM_BLK - 1:CUM_BLK, :]

    def store_cum(rows, c):
        cum_ref[rows, :] = c

    cumsum_rows(lambda rows: eq[rows].astype(F32), store_cum)
    sel = gt | (eq & (cum_ref[...] <= need))
    self32 = sel.astype(F32)
    cumsum_rows(lambda rows: self32[rows], store_cum)
    cum = cum_ref[...]
    pos_ref[0] = jnp.where(sel, cum.astype(I32) - 1, NEG_POS)
    wsel_ref[0] = jnp.where(sel, aff, 0.0)

    nstart = starts_ref.shape[1]
    tblk = t // (nstart - 1)
    starts_ref[0, 0:1, :] = jnp.zeros((1, V7X_LANES), I32)
    for tb in range(1, nstart):
        starts_ref[0, tb:tb + 1, :] = cum[tb * tblk - 1:tb * tblk, :].astype(I32)

    jj = lax.broadcasted_iota(I32, (1, cap), 1).astype(F32)
    ones = jnp.ones((8, CUM_BLK), BF16)
    for e in range(N_EXPERTS):
        acc = jnp.zeros((8, cap), F32)
        for blk in range(t // CUM_BLK):
            col = cum_ref[blk * CUM_BLK:(blk + 1) * CUM_BLK, e:e + 1]
            acc = acc + jnp.dot(ones, (col <= jj).astype(BF16), preferred_element_type=F32)
        idx_ref[0, e:e + 1, :] = acc[0:1, :].astype(I32)


def _select(aff, bsz, t, n_tok_blk):
    cap = CAPACITY_FACTOR * t // N_EXPERTS
    aff3 = aff.reshape(bsz, t, V7X_LANES)
    tl = pl.BlockSpec((1, t, V7X_LANES), lambda b: (b, 0, 0))
    return pl.pallas_call(
        functools.partial(_select_kernel, cap),
        grid=(bsz,),
        in_specs=[tl],
        out_specs=[pl.BlockSpec((1, N_EXPERTS, cap), lambda b: (b, 0, 0)), tl, tl,
                   pl.BlockSpec((1, n_tok_blk + 1, V7X_LANES), lambda b: (b, 0, 0))],
        out_shape=[jax.ShapeDtypeStruct((bsz, N_EXPERTS, cap), I32),
                   jax.ShapeDtypeStruct((bsz, t, V7X_LANES), I32),
                   jax.ShapeDtypeStruct((bsz, t, V7X_LANES), F32),
                   jax.ShapeDtypeStruct((bsz, n_tok_blk + 1, V7X_LANES), I32)],
        scratch_shapes=[pltpu.VMEM((t, V7X_LANES), F32)],
        compiler_params=_cparams(("parallel",), V7X_VMEM_LIMIT_BYTES),
        name="moe_select",
    )(aff3)


def _ffn_kernel(nf, nd, nbg, cap, cur_ref, nxt_ref, h_hbm, wg_ref, wu_ref, wd_ref, y_ref,
                xin, xb, hid, sem):
    s = pl.program_id(2)
    group = pl.program_id(0) * pl.num_programs(1) + pl.program_id(1)
    ngroups = pl.num_programs(0) * pl.num_programs(1)
    rows = nbg * cap
    per = rows // (nf + nd)
    tf = EXPERT_FF // nf

    def row_copy(idx_ref, j):
        return pltpu.make_async_copy(h_hbm.at[pl.ds(idx_ref[0, 0, j], 1), :],
                                     xin.at[pl.ds(j, 1), :], sem)

    def start_rows(idx_ref, base, n):
        def body(j, c):
            row_copy(idx_ref, base + j).start()
            return c
        lax.fori_loop(0, n, body, 0, unroll=8)

    def wait_rows(idx_ref, n):
        def body(j, c):
            row_copy(idx_ref, j).wait()
            return c
        lax.fori_loop(0, n, body, 0, unroll=8)

    @pl.when(s == 0)
    def _():
        @pl.when(group == 0)
        def _():
            start_rows(cur_ref, 0, rows)
        wait_rows(cur_ref, rows)
        xb[...] = xin[...].astype(BF16)

    def prefetch_rows():
        for j in range(per):
            row_copy(nxt_ref, s * per + j).start()

    @pl.when(s < nf)
    def _():
        prefetch_rows()
        x = xb[...]
        g = jnp.dot(x, wg_ref[0].astype(BF16), preferred_element_type=F32)
        u = jnp.dot(x, wu_ref[0].astype(BF16), preferred_element_type=F32)
        hid[s] = (g * jax.nn.sigmoid(g) * u).astype(BF16)

    @pl.when(s >= nf)
    def _():
        prefetch_rows()
        w = wd_ref[0].astype(BF16)
        acc = None
        for f in range(nf):
            part = jnp.dot(hid[f], w[f * tf:(f + 1) * tf, :], preferred_element_type=F32)
            acc = part if acc is None else acc + part
        for bi in range(nbg):
            y_ref[bi, 0] = acc[bi * cap:(bi + 1) * cap, :].astype(y_ref.dtype)

    @pl.when((group + 1 == ngroups) & (s + 1 == nf + nd))
    def _():
        wait_rows(nxt_ref, rows)


def _expert_ffn(idx, h, w_gate, w_up, w_down, layer, bsz, t, nf, nd):
    cap = idx.shape[-1]
    d = h.shape[1]
    nbg = 2 if bsz % 2 == 0 else 1
    nh = bsz // nbg
    rows = nbg * cap
    tf, tn = EXPERT_FF // nf, d // nd
    gidx = idx + (jnp.arange(bsz, dtype=I32) * t)[:, None, None]
    gidx = gidx.reshape(nh, nbg, N_EXPERTS, cap).transpose(2, 0, 1, 3).reshape(N_EXPERTS * nh, 1, rows)
    last = N_EXPERTS * nh - 1
    return pl.pallas_call(
        functools.partial(_ffn_kernel, nf, nd, nbg, cap),
        grid=(N_EXPERTS, nh, nf + nd),
        in_specs=[pl.BlockSpec((1, 1, rows), lambda e, hf, s: (e * nh + hf, 0, 0),
                               memory_space=pltpu.SMEM),
                  pl.BlockSpec((1, 1, rows), lambda e, hf, s: (jnp.minimum(e * nh + hf + 1, last), 0, 0),
                               memory_space=pltpu.SMEM),
                  pl.BlockSpec(memory_space=pl.ANY),
                  pl.BlockSpec((None, 1, d, tf), lambda e, hf, s: (layer, e, 0, jnp.minimum(s, nf - 1))),
                  pl.BlockSpec((None, 1, d, tf), lambda e, hf, s: (layer, e, 0, jnp.minimum(s, nf - 1))),
                  pl.BlockSpec((None, 1, EXPERT_FF, tn),
                               lambda e, hf, s: (layer, e, 0, jnp.maximum(s - nf, 0)))],
        out_specs=pl.BlockSpec((nbg, 1, cap, tn), lambda e, hf, s: (hf, e, 0, jnp.maximum(s - nf, 0))),
        out_shape=jax.ShapeDtypeStruct((bsz, N_EXPERTS, cap, d), BF16),
        scratch_shapes=[pltpu.VMEM((rows, d), F32), pltpu.VMEM((rows, d), BF16),
                        pltpu.VMEM((nf, rows, tf), BF16), pltpu.SemaphoreType.DMA],
        compiler_params=_cparams(("arbitrary", "arbitrary", "arbitrary"), V7X_VMEM_LIMIT_BYTES),
        name="moe_ffn",
    )(gidx, gidx, h, w_gate, w_up, w_down)


def _combine_kernel(cap, tblk, starts_ref, x_ref, pos_ref, w_ref, y_hbm, o_ref, kbuf, sems):
    b = pl.program_id(0)
    tb = pl.program_id(1)
    ntb = pl.num_programs(1)
    step = b * ntb + tb
    cur = step % 2
    o_ref[0] = x_ref[0]

    def firsts(bb, tt):
        return [(starts_ref[bb, tt, e] // 16) * 16 for e in range(N_EXPERTS)]

    def offsets(first, p):
        lows = [first[e] + p * ROW_WIN for e in range(N_EXPERTS)]
        return lows, [pl.multiple_of(jnp.minimum(lo, cap - ROW_WIN), 16) for lo in lows]

    def window_copy(bb, e, off, buf):
        return pltpu.make_async_copy(y_hbm.at[bb, e, pl.ds(off, ROW_WIN), :],
                                     kbuf.at[buf, pl.ds(e * ROW_WIN, ROW_WIN), :], sems.at[buf])

    def start_pass(bb, first, p, buf):
        _, offs = offsets(first, p)
        for e in range(N_EXPERTS):
            window_copy(bb, e, offs[e], buf).start()

    def finish_pass(first, p, buf):
        lows, offs = offsets(first, p)
        lane = lax.broadcasted_iota(I32, (tblk, V7X_LANES), 1)
        slot = lane % ROW_WIN
        head = lane < ROW_WIN
        tiles = []
        for g in range(N_EXPERTS * ROW_WIN // V7X_LANES):
            ea, eb = 2 * g, 2 * g + 1
            pe = jnp.where(head, pos_ref[0, :, ea:ea + 1], pos_ref[0, :, eb:eb + 1])
            we = jnp.where(head, w_ref[0, :, ea:ea + 1], w_ref[0, :, eb:eb + 1])
            off = jnp.where(head, offs[ea], offs[eb])
            low = jnp.where(head, lows[ea], lows[eb])
            hit = (pe == off + slot) & (pe >= low)
            tiles.append(jnp.where(hit, we, 0.0).astype(BF16))
        pmat = jnp.concatenate(tiles, axis=1)
        for e in range(N_EXPERTS):
            window_copy(b, e, offs[e], buf).wait()
        o_ref[0] += jnp.dot(pmat, kbuf[buf], preferred_element_type=F32)

    first = firsts(b, tb)
    npass = jnp.int32(1)
    for e in range(N_EXPERTS):
        npass = jnp.maximum(npass, (starts_ref[b, tb + 1, e] - first[e] + ROW_WIN - 1) // ROW_WIN)

    @pl.when(step == 0)
    def _():
        start_pass(b, first, 0, cur)

    @pl.when(step + 1 < pl.num_programs(0) * ntb)
    def _():
        wrap = tb + 1 == ntb
        nb = jnp.where(wrap, b + 1, b)
        nt = jnp.where(wrap, 0, tb + 1)
        start_pass(nb, firsts(nb, nt), 0, 1 - cur)

    finish_pass(first, 0, cur)

    def extra_pass(p, carry):
        start_pass(b, first, p, cur)
        finish_pass(first, p, cur)
        return carry

    lax.fori_loop(1, npass, extra_pass, 0)


def _combine(x, pos, wsel, starts, y, tblk):
    bsz, t, d = x.shape
    cap = y.shape[2]
    return pl.pallas_call(
        functools.partial(_combine_kernel, cap, tblk),
        grid=(bsz, t // tblk),
        in_specs=[pl.BlockSpec(memory_space=pltpu.SMEM),
                  pl.BlockSpec((1, tblk, d), lambda b, i: (b, i, 0)),
                  pl.BlockSpec((1, tblk, V7X_LANES), lambda b, i: (b, i, 0)),
                  pl.BlockSpec((1, tblk, V7X_LANES), lambda b, i: (b, i, 0)),
                  pl.BlockSpec(memory_space=pl.ANY)],
        out_specs=pl.BlockSpec((1, tblk, d), lambda b, i: (b, i, 0)),
        out_shape=jax.ShapeDtypeStruct((bsz, t, d), F32),
        scratch_shapes=[pltpu.VMEM((2, N_EXPERTS * ROW_WIN, d), BF16), pltpu.SemaphoreType.DMA((2,))],
        compiler_params=_cparams(("arbitrary", "arbitrary")),
        name="moe_combine",
    )(starts, x, pos, wsel, y)


def _tiles(n, t):
    pick = lambda want, total: want if total % want == 0 else total
    return dict(
        tm_proj=pick(1024, n), tn_proj=768,
        tq=pick(2048, t), tk=pick(512, t), lb=pick(512, t),
        tm_gsum=pick(1024, n), tn_gsum=256, tm_oproj=pick(512, n),
        nf_ffn=4, nd_ffn=4,
        tblk=pick(256, t),
    )


def kernel(x, positions, norm_mix, w_in, conv_a, q_norm, k_norm, lambda_q1, lambda_k1, lambda_q2, lambda_k2, subln, conv_c, a_log_f, a_log_b, dt_bias_f, dt_bias_b, o_norm, w_out_a, w_out_b, w_out_c, w_gate, b_gate, w_o, norm_ffn, w_router, w_e_gate, w_e_up, w_e_down):
    bsz, t, d = x.shape
    n = bsz * t
    depth = w_in.shape[0]
    ts = _tiles(n, t)
    tables = _rope_tables(positions)
    pad_small = jnp.zeros((d, V7X_LANES - N_SMALL), F32)
    tile2 = lambda g: jnp.concatenate([g, g]).reshape(1, V7X_LANES)
    pad64 = lambda vv: jnp.concatenate([vv, jnp.zeros_like(vv)])
    w_in_bf = w_in.astype(BF16)

    xf = x.reshape(n, d)
    for l in range(depth):
        lam_init = 0.8 - 0.6 * math.exp(-0.3 * l)
        w_small = jnp.concatenate([w_in[l, :, N_MAIN:], pad_small], axis=1)
        main, small = _norm_matmul(xf, norm_mix[l], w_in_bf, l, N_MAIN, w_small,
                                   ts["tm_proj"], ts["tn_proj"])
        main3 = main.reshape(bsz, t, N_MAIN)
        small3 = small.reshape(bsz, t, V7X_LANES)

        mix_a = _mixer_a(main3, conv_a[l])
        lamv = jnp.stack([pad64(lambda_q1[l]), pad64(lambda_k1[l]),
                          pad64(lambda_q2[l]), pad64(lambda_k2[l])])
        mix_b = _diff_attention(main3, tables, tile2(q_norm[l]), tile2(k_norm[l]), lamv,
                                subln[l].reshape(1, V7X_LANES), lam_init, ts["tq"], ts["tk"])
        qkv_c = _gdn_prep(main3, conv_c[l])
        o_f, o_b = _gdn(qkv_c, small3, a_log_f[l], dt_bias_f[l], a_log_b[l], dt_bias_b[l], ts["lb"])

        w_out = jnp.concatenate([w_out_a[l], w_out_b[l], w_out_c[l]], axis=0).astype(BF16)
        gsum = _gated_sum(xf, norm_mix[l], mix_a.reshape(n, W_A), mix_b.reshape(n, W_B),
                          o_f.reshape(n, W_C), o_b.reshape(n, W_C), main, o_norm[l],
                          w_out, w_gate[l].astype(BF16), b_gate[l].reshape(1, N_BRANCH * d),
                          ts["tm_gsum"], ts["tn_gsum"])
        x1 = _out_proj(xf, gsum, w_o[l].astype(BF16), ts["tm_oproj"])
        h, aff = _router(x1, norm_ffn[l], w_router[l], ts["tm_oproj"])
        idx, pos, wsel, starts = _select(aff, bsz, t, t // ts["tblk"])
        y = _expert_ffn(idx, h, w_e_gate, w_e_up, w_e_down, l, bsz, t, ts["nf_ffn"], ts["nd_ffn"])
        xf = _combine(x1.reshape(bsz, t, d), pos, wsel, starts, y, ts["tblk"]).reshape(n, d)
    return xf.reshape(bsz, t, d)
```

```python
import functools
import math

import jax
import jax.numpy as jnp
from jax import lax
from jax.experimental import pallas as pl
from jax.experimental.pallas import tpu as pltpu

F32 = jnp.float32
BF16 = jnp.bfloat16
I32 = jnp.int32

D_MODEL = 2048
HEAD_DIM = 128
W_A = D_MODEL // 4
DH_B = 64
DV_B = 2 * DH_B
H_B = (3 * D_MODEL // 8) // DV_B
W_B = H_B * DV_B
ROT_DIM = DH_B // 4
ROPE_THETA = 500000.0
DK_C = HEAD_DIM
DV_C = HEAD_DIM
H_C = (3 * D_MODEL // 8) // DV_C
W_C = H_C * DV_C
CHUNK = 64
N_BRANCH = 3
N_EXPERTS = 16
CAPACITY_FACTOR = 2
EXPERT_FF = D_MODEL // 2
EPS = 1e-6

N_MAIN = 3 * W_A + 3 * W_B + 4 * W_C
N_SMALL = 4 * H_C
COL_A = 0
COL_B = 3 * W_A
COL_C = COL_B + 3 * W_B

V7X_LANES = 128
BF16_SUBLANES = 16
V7X_VMEM_LIMIT_BYTES = 56 * 1024 * 1024
ROW_WIN = 64
NEG_POS = -(1 << 20)


def _cparams(sem, vmem=None):
    return pltpu.CompilerParams(dimension_semantics=sem, vmem_limit_bytes=vmem)


def _split_dot(a, b):
    hi = a.astype(BF16)
    lo = (a - hi.astype(F32)).astype(BF16)
    return (jnp.dot(hi, b, preferred_element_type=F32)
            + jnp.dot(lo, b, preferred_element_type=F32))


def _norm_matmul_kernel(x_ref, g_ref, w_ref, ws_ref, o_ref, os_ref, xn_ref):
    @pl.when(pl.program_id(1) == 0)
    def _():
        x = x_ref[...]
        xn = x * lax.rsqrt(jnp.mean(x * x, axis=-1, keepdims=True) + EPS) * g_ref[...]
        xn_ref[...] = xn.astype(BF16)
        os_ref[...] = jnp.dot(xn_ref[...], ws_ref[...].astype(BF16), preferred_element_type=F32)

    o_ref[...] = lax.dot_general(xn_ref[...], w_ref[...].astype(BF16), (((1,), (1,)), ((), ())),
                                 preferred_element_type=F32).astype(o_ref.dtype)


def _norm_matmul(x, gain, w, layer, nout, w_small, tm, tn):
    n, d = x.shape
    return pl.pallas_call(
        _norm_matmul_kernel,
        grid=(n // tm, nout // tn),
        in_specs=[pl.BlockSpec((tm, d), lambda i, j: (i, 0)),
                  pl.BlockSpec((1, d), lambda i, j: (0, 0)),
                  pl.BlockSpec((None, tn, d), lambda i, j: (layer, j, 0)),
                  pl.BlockSpec((d, V7X_LANES), lambda i, j: (0, 0))],
        out_specs=[pl.BlockSpec((tm, tn), lambda i, j: (i, j)),
                   pl.BlockSpec((tm, V7X_LANES), lambda i, j: (i, 0))],
        out_shape=[jax.ShapeDtypeStruct((n, nout), BF16),
                   jax.ShapeDtypeStruct((n, V7X_LANES), F32)],
        scratch_shapes=[pltpu.VMEM((tm, d), BF16)],
        compiler_params=_cparams(("parallel", "arbitrary"), V7X_VMEM_LIMIT_BYTES),
        name="norm_matmul",
    )(x, gain.reshape(1, d), w, w_small)


def _conv3(x, w):
    t = x.shape[0]
    row = lax.broadcasted_iota(I32, x.shape, 0)
    prev = jnp.where(row == 0, 0.0, pltpu.roll(x, 1, 0))
    nxt = jnp.where(row == t - 1, 0.0, pltpu.roll(x, t - 1, 0))
    return prev * w[0:1, :] + x * w[1:2, :] + nxt * w[2:3, :]


def _mixer_a_kernel(b_ref, c_ref, v_ref, w_ref, o_ref):
    cv = c_ref[0].astype(F32) * v_ref[0].astype(F32)
    o_ref[0] = (b_ref[0].astype(F32) * _conv3(cv, w_ref[...])).astype(o_ref.dtype)


def _mixer_a(main, conv_w):
    bsz, t, _ = main.shape
    nblk = W_A // V7X_LANES
    c0 = COL_A // V7X_LANES
    spec = lambda off: pl.BlockSpec((1, t, V7X_LANES), lambda b, c: (b, 0, c0 + off + c))
    return pl.pallas_call(
        _mixer_a_kernel,
        grid=(bsz, nblk),
        in_specs=[spec(0), spec(nblk), spec(2 * nblk),
                  pl.BlockSpec((3, V7X_LANES), lambda b, c: (0, c))],
        out_specs=pl.BlockSpec((1, t, V7X_LANES), lambda b, c: (b, 0, c)),
        out_shape=jax.ShapeDtypeStruct((bsz, t, W_A), BF16),
        compiler_params=_cparams(("parallel", "parallel")),
        name="mixer_a",
    )(main, main, main, conv_w)


def _gdn_prep_kernel(x_ref, w_ref, o_ref):
    c = pl.program_id(1)
    y = _conv3(x_ref[0].astype(F32), w_ref[...])
    y = y * jax.nn.sigmoid(y)
    inv = lax.rsqrt(jnp.sum(y * y, axis=-1, keepdims=True) + EPS)
    scale = jnp.where(c < H_C, inv * (DK_C ** -0.5), jnp.where(c < 2 * H_C, inv, 1.0))
    o_ref[0] = (y * scale).astype(o_ref.dtype)


def _gdn_prep(main, conv_w):
    bsz, t, _ = main.shape
    nblk = 3 * W_C // V7X_LANES
    c0 = COL_C // V7X_LANES
    return pl.pallas_call(
        _gdn_prep_kernel,
        grid=(bsz, nblk),
        in_specs=[pl.BlockSpec((1, t, V7X_LANES), lambda b, c: (b, 0, c0 + c)),
                  pl.BlockSpec((3, V7X_LANES), lambda b, c: (0, c))],
        out_specs=pl.BlockSpec((1, t, V7X_LANES), lambda b, c: (b, 0, c)),
        out_shape=jax.ShapeDtypeStruct((bsz, t, 3 * W_C), BF16),
        compiler_params=_cparams(("parallel", "parallel")),
        name="gdn_prep",
    )(main, conv_w)


def _rope_kernel(pos_ref, c_ref, s1_ref, s2_ref):
    pos = pos_ref[0].astype(F32)
    lane = lax.broadcasted_iota(I32, (1, V7X_LANES), 1)
    l64 = lane % DH_B
    half = ROT_DIM // 2
    fidx = jnp.where(l64 < half, l64, l64 - half).astype(F32)
    inv_freq = jnp.exp(fidx * (-math.log(ROPE_THETA) * 2.0 / ROT_DIM))
    ang = pos * inv_freq
    cos, sin = jnp.cos(ang), jnp.sin(ang)
    first = l64 < half
    second = (l64 >= half) & (l64 < ROT_DIM)
    c_ref[0] = jnp.where(first | second, cos, 1.0)
    s1_ref[0] = jnp.where(first, -sin, 0.0)
    s2_ref[0] = jnp.where(second, sin, 0.0)


def _rope_tables(positions):
    bsz, t = positions.shape
    shp = jax.ShapeDtypeStruct((bsz, t, V7X_LANES), F32)
    spec = pl.BlockSpec((1, t, V7X_LANES), lambda b: (b, 0, 0))
    return pl.pallas_call(
        _rope_kernel,
        grid=(bsz,),
        in_specs=[pl.BlockSpec((1, t, 1), lambda b: (b, 0, 0))],
        out_specs=[spec, spec, spec],
        out_shape=[shp, shp, shp],
        compiler_params=_cparams(("parallel",)),
        name="rope_tables",
    )(positions.reshape(bsz, t, 1))


def _qk_prep(x, gain, cos, s1, s2):
    r = lax.broadcasted_iota(I32, (V7X_LANES, V7X_LANES), 0) // DH_B
    c = lax.broadcasted_iota(I32, (V7X_LANES, V7X_LANES), 1) // DH_B
    blockdiag = (r == c).astype(BF16)
    ssq = _split_dot(x * x, blockdiag)
    xn = x * lax.rsqrt(ssq * (1.0 / DH_B) + EPS) * gain
    half = ROT_DIM // 2
    return xn * cos + pltpu.roll(xn, V7X_LANES - half, 1) * s1 + pltpu.roll(xn, half, 1) * s2


def _attn_kernel(lam_init, tq, tk, q_ref, k_ref, v_ref, c_ref, s1_ref, s2_ref,
                 qg_ref, kg_ref, lamv_ref, sub_ref, o_ref, kp_ref, vx_ref, acc_ref):
    qi = pl.program_id(2)
    t = kp_ref.shape[0]

    @pl.when(qi == 0)
    def _():
        kp_ref[...] = _qk_prep(k_ref[0].astype(F32), kg_ref[...],
                               c_ref[0], s1_ref[0], s2_ref[0]).astype(BF16)
        vx_ref[:DV_B, :] = v_ref[0].astype(F32).T.astype(BF16)
        vx_ref[DV_B:, :] = jnp.ones((vx_ref.shape[0] - DV_B, t), BF16)

    rows = pl.ds(pl.multiple_of(qi * tq, tq), tq)
    q = _qk_prep(q_ref[0].astype(F32), qg_ref[...], c_ref[0, rows, :], s1_ref[0, rows, :],
                 s2_ref[0, rows, :]) * (DH_B ** -0.5 * math.log2(math.e))
    lane = lax.broadcasted_iota(I32, (tq, V7X_LANES), 1)
    qt = [jnp.where(keep, q, 0.0).T.astype(BF16) for keep in (lane < DH_B, lane >= DH_B)]

    def scores(j):
        kj = kp_ref[j * tk:(j + 1) * tk, :]
        return [jnp.dot(kj, qc, preferred_element_type=F32) for qc in qt]

    acc_ref[...] = jnp.zeros_like(acc_ref)
    m = [jnp.full((1, tq), -jnp.inf, F32)] * 2
    nkb = t // tk
    ahead = 2
    pending = [scores(j) for j in range(min(ahead, nkb))]
    for j in range(nkb):
        cur = pending.pop(0)
        if j + ahead < nkb:
            pending.append(scores(j + ahead))
        vj = vx_ref[:, j * tk:(j + 1) * tk]
        for c in range(2):
            m_new = jnp.maximum(m[c], jnp.max(cur[c], axis=0, keepdims=True))
            e = jnp.exp2(cur[c] - m_new).astype(BF16)
            acc_ref[c] = acc_ref[c] * jnp.exp2(m[c] - m_new) + jnp.dot(vj, e, preferred_element_type=F32)
            m[c] = m_new
    pv = [acc_ref[c, :DV_B, :] for c in range(2)]
    l = [acc_ref[c, DV_B:DV_B + 1, :] for c in range(2)]
    lv = lamv_ref[...]
    lam = (jnp.exp(jnp.sum(lv[0:1] * lv[1:2], axis=-1, keepdims=True))
           - jnp.exp(jnp.sum(lv[2:3] * lv[3:4], axis=-1, keepdims=True)) + lam_init)
    o = (pv[0] / l[0] - lam * (pv[1] / l[1])).T
    o = o * lax.rsqrt(jnp.mean(o * o, axis=-1, keepdims=True) + EPS) * sub_ref[...]
    o_ref[0] = (o * (1.0 - lam_init)).astype(o_ref.dtype)


def _diff_attention(main, tables, q_gain, k_gain, lamv, subln, lam_init, tq, tk):
    bsz, t, _ = main.shape
    cq = COL_B // V7X_LANES
    ck = cq + H_B
    cv = ck + H_B
    tab = pl.BlockSpec((1, t, V7X_LANES), lambda b, h, i: (b, 0, 0))
    vec = pl.BlockSpec((1, V7X_LANES), lambda b, h, i: (0, 0))
    return pl.pallas_call(
        functools.partial(_attn_kernel, lam_init, tq, tk),
        grid=(bsz, H_B, t // tq),
        in_specs=[pl.BlockSpec((1, tq, V7X_LANES), lambda b, h, i: (b, i, cq + h)),
                  pl.BlockSpec((1, t, V7X_LANES), lambda b, h, i: (b, 0, ck + h)),
                  pl.BlockSpec((1, t, V7X_LANES), lambda b, h, i: (b, 0, cv + h)),
                  tab, tab, tab, vec, vec,
                  pl.BlockSpec((4, V7X_LANES), lambda b, h, i: (0, 0)),
                  vec],
        out_specs=pl.BlockSpec((1, tq, V7X_LANES), lambda b, h, i: (b, i, h)),
        out_shape=jax.ShapeDtypeStruct((bsz, t, W_B), BF16),
        scratch_shapes=[pltpu.VMEM((t, V7X_LANES), BF16),
                        pltpu.VMEM((DV_B + BF16_SUBLANES, t), BF16),
                        pltpu.VMEM((2, DV_B + BF16_SUBLANES, tq), F32)],
        compiler_params=_cparams(("parallel", "parallel", "arbitrary"), V7X_VMEM_LIMIT_BYTES),
        name="diff_attention",
    )(main, main, main, *tables, q_gain, k_gain, lamv, subln)


def _gdn_kernel(lb, qf_ref, kf_ref, vf_ref, smf_ref, qb_ref, kb_ref, vb_ref, smb_ref,
                alf_ref, dtf_ref, alb_ref, dtb_ref, of_ref, ob_ref, s_ref):
    @pl.when(pl.program_id(1) == 0)
    def _():
        s_ref[...] = jnp.zeros_like(s_ref)

    nchunk = lb // CHUNK
    ri = lax.broadcasted_iota(I32, (CHUNK, CHUNK), 0)
    ci = lax.broadcasted_iota(I32, (CHUNK, CHUNK), 1)
    eye = (ri == ci).astype(F32)
    sides = []
    for rev, refs in ((False, (qf_ref, kf_ref, vf_ref, smf_ref, alf_ref, dtf_ref, of_ref)),
                      (True, (qb_ref, kb_ref, vb_ref, smb_ref, alb_ref, dtb_ref, ob_ref))):
        incl = (ri <= ci) if rev else (ri >= ci)
        sides.append(dict(
            q=refs[0], k=refs[1], v=refs[2], sm=refs[3], alog=refs[4], dtb=refs[5], o=refs[6],
            incl=incl, strict=(ri < ci) if rev else (ri > ci), cum_mat=incl.astype(BF16),
            last=0 if rev else CHUNK - 1,
            off_b=H_C if rev else 0,
            off_a=3 * H_C if rev else 2 * H_C,
            order=list(range(nchunk - 1, -1, -1) if rev else range(nchunk))))
    units = [(si, sides[si]["order"][k], h) for k in range(nchunk) for si in range(2) for h in range(H_C)]
    nt = (((1,), (1,)), ((), ()))
    dot = functools.partial(jnp.dot, preferred_element_type=F32)

    gates = {}
    for si, sd in enumerate(sides):
        for c in sd["order"]:
            rows = slice(c * CHUNK, (c + 1) * CHUNK)
            sm = sd["sm"][0, rows, :]
            g_all = -jnp.exp(sd["alog"][...]) * jax.nn.softplus(sm + sd["dtb"][...])
            gc_all = _split_dot_left(sd["cum_mat"], g_all)
            gates[si, c] = (jax.nn.sigmoid(sm), gc_all, gc_all.T)

    pre = {}
    for si, c, h in units:
        sd = sides[si]
        rows = slice(c * CHUNK, (c + 1) * CHUNK)
        lanes = slice(h * HEAD_DIM, (h + 1) * HEAD_DIM)
        beta_all, gc_all, gct_all = gates[si, c]
        q = sd["q"][0, rows, lanes].astype(F32)
        k = sd["k"][0, rows, lanes]
        kf = k.astype(F32)
        beta = beta_all[:, sd["off_b"] + h:sd["off_b"] + h + 1]
        gcol = gc_all[:, sd["off_a"] + h:sd["off_a"] + h + 1]
        grow = gct_all[sd["off_a"] + h:sd["off_a"] + h + 1, :]
        glast = gcol[sd["last"]:sd["last"] + 1, :]
        decay = jnp.where(sd["incl"], jnp.exp(jnp.where(sd["incl"], gcol - grow, 0.0)), 0.0)
        kb = kf * beta
        eg = jnp.exp(gcol)
        rhs = jnp.concatenate([sd["v"][0, rows, lanes].astype(F32) * beta, kb * eg], axis=-1).astype(BF16)
        pre[si, c, h] = dict(k=k, kb=kb, decay=decay, rhs=rhs, glast=glast,
                             q=q, q_dec=q * eg, k_dec=(kf * jnp.exp(glast - gcol)).astype(BF16))

    for u in units:
        d = pre[u]
        res = lax.dot_general(jnp.concatenate([d["kb"], d["q"]], axis=0).astype(BF16), d["k"], nt,
                              preferred_element_type=F32)
        d["low"] = jnp.where(sides[u[0]]["strict"], res[:CHUNK] * d["decay"], 0.0)
        d["attn"] = (res[CHUNK:] * d["decay"]).astype(BF16)
    for u in units:
        d = pre[u]
        lb16 = d["low"].astype(BF16)
        d["n"] = dot(lb16, lb16)
        d["p"] = eye - d["low"]
    for r in range(1, 6):
        for u in units:
            d = pre[u]
            nb16 = d["n"].astype(BF16)
            if r < 5:
                res = dot(jnp.concatenate([d["n"], d["p"]], axis=0).astype(BF16), nb16)
                d["n"] = res[:CHUNK]
                d["p"] = d["p"] + res[CHUNK:]
            else:
                d["p"] = d["p"] + dot(d["p"].astype(BF16), nb16)
    for u in units:
        d = pre[u]
        d["sol"] = dot(d["p"].astype(BF16), d["rhs"])

    chains = [(si, h) for si in range(2) for h in range(H_C)]
    st = {ch: s_ref[ch[0], ch[1]] for ch in chains}
    for kk in range(nchunk):
        res, vb = {}, {}
        for si, h in chains:
            d = pre[si, sides[si]["order"][kk], h]
            lhs = jnp.concatenate([d["sol"][:, DV_C:], d["q_dec"]], axis=0).astype(BF16)
            res[si, h] = dot(lhs, st[si, h].astype(BF16))
        for si, h in chains:
            d = pre[si, sides[si]["order"][kk], h]
            vb[si, h] = (d["sol"][:, :DV_C] - res[si, h][:CHUNK]).astype(BF16)
        for si, h in chains:
            c = sides[si]["order"][kk]
            d = pre[si, c, h]
            o_ref = sides[si]["o"]
            o_ref[0, c * CHUNK:(c + 1) * CHUNK, h * HEAD_DIM:(h + 1) * HEAD_DIM] = (
                res[si, h][CHUNK:] + dot(d["attn"], vb[si, h])).astype(o_ref.dtype)
            st[si, h] = st[si, h] * jnp.exp(d["glast"]) + lax.dot_general(
                d["k_dec"], vb[si, h], (((0,), (0,)), ((), ())), preferred_element_type=F32)
    for si, h in chains:
        s_ref[si, h] = st[si, h]


def _split_dot_left(a, b):
    hi = b.astype(BF16)
    lo = (b - hi.astype(F32)).astype(BF16)
    return (jnp.dot(a, hi, preferred_element_type=F32)
            + jnp.dot(a, lo, preferred_element_type=F32))


def _gdn(qkv, small, a_log_f, dt_bias_f, a_log_b, dt_bias_b, lb):
    bsz, t, _ = qkv.shape
    nb = t // lb
    fwd = lambda i: i
    bwd = lambda i: nb - 1 - i

    def pad(vec, rev):
        off_a = 3 * H_C if rev else 2 * H_C
        return jnp.zeros((1, V7X_LANES), F32).at[0, off_a:off_a + H_C].set(vec)

    col = lambda j, blk: pl.BlockSpec((1, lb, W_C), lambda b, i: (b, blk(i), j))
    sm = lambda blk: pl.BlockSpec((1, lb, V7X_LANES), lambda b, i: (b, blk(i), 0))
    vec = pl.BlockSpec((1, V7X_LANES), lambda b, i: (0, 0))
    out = jax.ShapeDtypeStruct((bsz, t, W_C), BF16)
    return pl.pallas_call(
        functools.partial(_gdn_kernel, lb),
        grid=(bsz, nb),
        in_specs=[col(0, fwd), col(1, fwd), col(2, fwd), sm(fwd),
                  col(0, bwd), col(1, bwd), col(2, bwd), sm(bwd),
                  vec, vec, vec, vec],
        out_specs=[col(0, fwd), col(0, bwd)],
        out_shape=[out, out],
        scratch_shapes=[pltpu.VMEM((2, H_C, DK_C, DV_C), F32)],
        compiler_params=_cparams(("parallel", "arbitrary"), V7X_VMEM_LIMIT_BYTES),
        name="gdn",
    )(qkv, qkv, qkv, small, qkv, qkv, qkv, small,
      pad(a_log_f, False), pad(dt_bias_f, False), pad(a_log_b, True), pad(dt_bias_b, True))


def _gated_sum_kernel(x_ref, gn_ref, ma_ref, mb_ref, of_ref, ob_ref, gc_ref, og_ref,
                      wout_ref, wg0_ref, wg1_ref, wg2_ref, bg0_ref, bg1_ref, bg2_ref,
                      o_ref, xn_ref, mix_ref):
    j = pl.program_id(1)

    @pl.when(j == 0)
    def _():
        x = x_ref[...]
        xn = x * lax.rsqrt(jnp.mean(x * x, axis=-1, keepdims=True) + EPS) * gn_ref[...]
        xn_ref[...] = xn.astype(BF16)
        mix_ref[:, 0:W_A] = ma_ref[...]
        mix_ref[:, W_A:W_A + W_B] = mb_ref[...]
        for h in range(H_C):
            lanes = slice(h * HEAD_DIM, (h + 1) * HEAD_DIM)
            o = of_ref[:, lanes].astype(F32) + ob_ref[:, lanes].astype(F32)
            o = o * lax.rsqrt(jnp.mean(o * o, axis=-1, keepdims=True) + EPS) * og_ref[...]
            g = gc_ref[:, lanes].astype(F32)
            c0 = W_A + W_B + h * HEAD_DIM
            mix_ref[:, c0:c0 + HEAD_DIM] = (o * (g * jax.nn.sigmoid(g))).astype(BF16)

    xn = xn_ref[...]
    s = None
    for lo, hi, wg_ref, bg_ref in ((0, W_A, wg0_ref, bg0_ref),
                                   (W_A, W_A + W_B, wg1_ref, bg1_ref),
                                   (W_A + W_B, D_MODEL, wg2_ref, bg2_ref)):
        y = jnp.dot(mix_ref[:, lo:hi], wout_ref[lo:hi, :], preferred_element_type=F32)
        gate = jax.nn.sigmoid(jnp.dot(xn, wg_ref[...], preferred_element_type=F32) + bg_ref[...])
        s = gate * y if s is None else s + gate * y
    o_ref[...] = s.astype(o_ref.dtype)


def _gated_sum(x, norm_gain, mix_a, mix_b, o_f, o_b, main, o_gain, w_out, w_gate, b_gate, tm, tn):
    n, d = x.shape
    nj = d // tn
    cg = (COL_C + 3 * W_C) // W_C
    row = lambda w: pl.BlockSpec((tm, w), lambda i, j: (i, 0))
    wg = lambda br: pl.BlockSpec((d, tn), lambda i, j: (0, br * nj + j))
    bg = lambda br: pl.BlockSpec((1, tn), lambda i, j: (0, br * nj + j))
    return pl.pallas_call(
        _gated_sum_kernel,
        grid=(n // tm, nj),
        in_specs=[row(d),
                  pl.BlockSpec((1, d), lambda i, j: (0, 0)),
                  row(W_A), row(W_B), row(W_C), row(W_C),
                  pl.BlockSpec((tm, W_C), lambda i, j: (i, cg)),
                  pl.BlockSpec((1, HEAD_DIM), lambda i, j: (0, 0)),
                  pl.BlockSpec((d, tn), lambda i, j: (0, j)),
                  wg(0), wg(1), wg(2), bg(0), bg(1), bg(2)],
        out_specs=pl.BlockSpec((tm, tn), lambda i, j: (i, j)),
        out_shape=jax.ShapeDtypeStruct((n, d), BF16),
        scratch_shapes=[pltpu.VMEM((tm, d), BF16), pltpu.VMEM((tm, d), BF16)],
        compiler_params=_cparams(("parallel", "arbitrary"), V7X_VMEM_LIMIT_BYTES),
        name="gated_sum",
    )(x, norm_gain.reshape(1, d), mix_a, mix_b, o_f, o_b, main, o_gain.reshape(1, HEAD_DIM),
      w_out, w_gate, w_gate, w_gate, b_gate, b_gate, b_gate)


def _out_proj_kernel(x_ref, s_ref, w_ref, o_ref):
    o_ref[...] = x_ref[...] + jnp.dot(s_ref[...], w_ref[...], preferred_element_type=F32)


def _out_proj(x, s, w_o, tm):
    n, d = x.shape
    row = pl.BlockSpec((tm, d), lambda i: (i, 0))
    return pl.pallas_call(
        _out_proj_kernel,
        grid=(n // tm,),
        in_specs=[row, row, pl.BlockSpec((d, d), lambda i: (0, 0))],
        out_specs=row,
        out_shape=jax.ShapeDtypeStruct((n, d), F32),
        compiler_params=_cparams(("parallel",), V7X_VMEM_LIMIT_BYTES),
        name="out_proj",
    )(x, s, w_o)


def _router_kernel(x_ref, g_ref, wh_ref, wl_ref, h_ref, aff_ref):
    x = x_ref[...]
    h = x * lax.rsqrt(jnp.mean(x * x, axis=-1, keepdims=True) + EPS) * g_ref[...]
    h_ref[...] = h
    hh = h.astype(BF16)
    hl = (h - hh.astype(F32)).astype(BF16)
    logits = (jnp.dot(hh, wh_ref[...], preferred_element_type=F32)
              + jnp.dot(hl, wh_ref[...], preferred_element_type=F32)
              + jnp.dot(hh, wl_ref[...], preferred_element_type=F32))
    lane = lax.broadcasted_iota(I32, logits.shape, 1)
    valid = lane < N_EXPERTS
    logits = jnp.where(valid, logits, -jnp.inf)
    m = jnp.max(logits, axis=-1, keepdims=True)
    e = jnp.where(valid, jnp.exp(logits - m), 0.0)
    aff_ref[...] = e / jnp.sum(e, axis=-1, keepdims=True)


def _router(x, gain, w_router, tm):
    n, d = x.shape
    wpad = jnp.zeros((d, V7X_LANES), F32).at[:, :N_EXPERTS].set(w_router)
    w_hi = wpad.astype(BF16)
    w_lo = (wpad - w_hi.astype(F32)).astype(BF16)
    wspec = pl.BlockSpec((d, V7X_LANES), lambda i: (0, 0))
    return pl.pallas_call(
        _router_kernel,
        grid=(n // tm,),
        in_specs=[pl.BlockSpec((tm, d), lambda i: (i, 0)),
                  pl.BlockSpec((1, d), lambda i: (0, 0)), wspec, wspec],
        out_specs=[pl.BlockSpec((tm, d), lambda i: (i, 0)),
                   pl.BlockSpec((tm, V7X_LANES), lambda i: (i, 0))],
        out_shape=[jax.ShapeDtypeStruct((n, d), F32),
                   jax.ShapeDtypeStruct((n, V7X_LANES), F32)],
        compiler_params=_cparams(("parallel",), V7X_VMEM_LIMIT_BYTES),
        name="router",
    )(x, gain.reshape(1, d), w_hi, w_lo)


CUM_BLK = 256


def _select_kernel(cap, aff_ref, idx_ref, pos_ref, wsel_ref, starts_ref, cum_ref):
    t = aff_ref.shape[1]
    aff = aff_ref[0]

    def search(i, lo):
        cand = lo | jnp.left_shift(jnp.int32(1), 30 - i)
        cnt = jnp.sum((aff >= pltpu.bitcast(cand, F32)).astype(I32), axis=0, keepdims=True)
        return jnp.where(cnt >= cap, cand, lo)

    thr_bits = lax.fori_loop(0, 31, search, jnp.zeros((1, V7X_LANES), I32))
    thr = pltpu.bitcast(thr_bits, F32)
    gt = aff > thr
    eq = aff == thr
    need = (cap - jnp.sum(gt.astype(I32), axis=0, keepdims=True)).astype(F32)

    ri = lax.broadcasted_iota(I32, (CUM_BLK, CUM_BLK), 0)
    ci = lax.broadcasted_iota(I32, (CUM_BLK, CUM_BLK), 1)
    tri = (ri >= ci).astype(BF16)

    def cumsum_rows(mask_f32_of_block, store):
        off = jnp.zeros((1, V7X_LANES), F32)
        for blk in range(t // CUM_BLK):
            rows = slice(blk * CUM_BLK, (blk + 1) * CUM_BLK)
            c = jnp.dot(tri, mask_f32_of_block(rows).astype(BF16), preferred_element_type=F32) + off
            store(rows, c)
            off = c[CUM_BLK - 1:CUM_BLK, :]

    def store_cum(rows, c):
        cum_ref[rows, :] = c

    cumsum_rows(lambda rows: eq[rows].astype(F32), store_cum)
    sel = gt | (eq & (cum_ref[...] <= need))
    self32 = sel.astype(F32)
    cumsum_rows(lambda rows: self32[rows], store_cum)
    cum = cum_ref[...]
    pos_ref[0] = jnp.where(sel, cum.astype(I32) - 1, NEG_POS)
    wsel_ref[0] = jnp.where(sel, aff, 0.0)

    nstart = starts_ref.shape[1]
    tblk = t // (nstart - 1)
    starts_ref[0, 0:1, :] = jnp.zeros((1, V7X_LANES), I32)
    for tb in range(1, nstart):
        starts_ref[0, tb:tb + 1, :] = cum[tb * tblk - 1:tb * tblk, :].astype(I32)

    jj = lax.broadcasted_iota(I32, (1, cap), 1).astype(F32)
    ones = jnp.ones((8, CUM_BLK), BF16)
    for e in range(N_EXPERTS):
        acc = jnp.zeros((8, cap), F32)
        for blk in range(t // CUM_BLK):
            col = cum_ref[blk * CUM_BLK:(blk + 1) * CUM_BLK, e:e + 1]
            acc = acc + jnp.dot(ones, (col <= jj).astype(BF16), preferred_element_type=F32)
        idx_ref[0, e:e + 1, :] = acc[0:1, :].astype(I32)


def _select(aff, bsz, t, n_tok_blk):
    cap = CAPACITY_FACTOR * t // N_EXPERTS
    aff3 = aff.reshape(bsz, t, V7X_LANES)
    tl = pl.BlockSpec((1, t, V7X_LANES), lambda b: (b, 0, 0))
    return pl.pallas_call(
        functools.partial(_select_kernel, cap),
        grid=(bsz,),
        in_specs=[tl],
        out_specs=[pl.BlockSpec((1, N_EXPERTS, cap), lambda b: (b, 0, 0)), tl, tl,
                   pl.BlockSpec((1, n_tok_blk + 1, V7X_LANES), lambda b: (b, 0, 0))],
        out_shape=[jax.ShapeDtypeStruct((bsz, N_EXPERTS, cap), I32),
                   jax.ShapeDtypeStruct((bsz, t, V7X_LANES), I32),
                   jax.ShapeDtypeStruct((bsz, t, V7X_LANES), F32),
                   jax.ShapeDtypeStruct((bsz, n_tok_blk + 1, V7X_LANES), I32)],
        scratch_shapes=[pltpu.VMEM((t, V7X_LANES), F32)],
        compiler_params=_cparams(("parallel",), V7X_VMEM_LIMIT_BYTES),
        name="moe_select",
    )(aff3)


def _ffn_kernel(nf, nd, nbg, cap, cur_ref, nxt_ref, h_hbm, wg_ref, wu_ref, wd_ref, y_ref,
                xin, xb, hid, sem):
    s = pl.program_id(2)
    group = pl.program_id(0) * pl.num_programs(1) + pl.program_id(1)
    ngroups = pl.num_programs(0) * pl.num_programs(1)
    rows = nbg * cap
    per = rows // (nf + nd)
    tf = EXPERT_FF // nf

    def row_copy(idx_ref, j):
        return pltpu.make_async_copy(h_hbm.at[pl.ds(idx_ref[0, 0, j], 1), :],
                                     xin.at[pl.ds(j, 1), :], sem)

    def start_rows(idx_ref, base, n):
        def body(j, c):
            row_copy(idx_ref, base + j).start()
            return c
        lax.fori_loop(0, n, body, 0, unroll=8)

    def wait_rows(idx_ref, n):
        def body(j, c):
            row_copy(idx_ref, j).wait()
            return c
        lax.fori_loop(0, n, body, 0, unroll=8)

    @pl.when(s == 0)
    def _():
        @pl.when(group == 0)
        def _():
            start_rows(cur_ref, 0, rows)
        wait_rows(cur_ref, rows)
        xb[...] = xin[...].astype(BF16)

    def prefetch_rows():
        for j in range(per):
            row_copy(nxt_ref, s * per + j).start()

    @pl.when(s < nf)
    def _():
        prefetch_rows()
        x = xb[...]
        g = jnp.dot(x, wg_ref[0].astype(BF16), preferred_element_type=F32)
        u = jnp.dot(x, wu_ref[0].astype(BF16), preferred_element_type=F32)
        hid[s] = (g * jax.nn.sigmoid(g) * u).astype(BF16)

    @pl.when(s >= nf)
    def _():
        prefetch_rows()
        w = wd_ref[0].astype(BF16)
        acc = None
        for f in range(nf):
            part = jnp.dot(hid[f], w[f * tf:(f + 1) * tf, :], preferred_element_type=F32)
            acc = part if acc is None else acc + part
        for bi in range(nbg):
            y_ref[bi, 0] = acc[bi * cap:(bi + 1) * cap, :].astype(y_ref.dtype)

    @pl.when((group + 1 == ngroups) & (s + 1 == nf + nd))
    def _():
        wait_rows(nxt_ref, rows)


def _expert_ffn(idx, h, w_gate, w_up, w_down, layer, bsz, t, nf, nd):
    cap = idx.shape[-1]
    d = h.shape[1]
    nbg = 2 if bsz % 2 == 0 else 1
    nh = bsz // nbg
    rows = nbg * cap
    tf, tn = EXPERT_FF // nf, d // nd
    gidx = idx + (jnp.arange(bsz, dtype=I32) * t)[:, None, None]
    gidx = gidx.reshape(nh, nbg, N_EXPERTS, cap).transpose(2, 0, 1, 3).reshape(N_EXPERTS * nh, 1, rows)
    last = N_EXPERTS * nh - 1
    return pl.pallas_call(
        functools.partial(_ffn_kernel, nf, nd, nbg, cap),
        grid=(N_EXPERTS, nh, nf + nd),
        in_specs=[pl.BlockSpec((1, 1, rows), lambda e, hf, s: (e * nh + hf, 0, 0),
                               memory_space=pltpu.SMEM),
                  pl.BlockSpec((1, 1, rows), lambda e, hf, s: (jnp.minimum(e * nh + hf + 1, last), 0, 0),
                               memory_space=pltpu.SMEM),
                  pl.BlockSpec(memory_space=pl.ANY),
                  pl.BlockSpec((None, 1, d, tf), lambda e, hf, s: (layer, e, 0, jnp.minimum(s, nf - 1))),
                  pl.BlockSpec((None, 1, d, tf), lambda e, hf, s: (layer, e, 0, jnp.minimum(s, nf - 1))),
                  pl.BlockSpec((None, 1, EXPERT_FF, tn),
                               lambda e, hf, s: (layer, e, 0, jnp.maximum(s - nf, 0)))],
        out_specs=pl.BlockSpec((nbg, 1, cap, tn), lambda e, hf, s: (hf, e, 0, jnp.maximum(s - nf, 0))),
        out_shape=jax.ShapeDtypeStruct((bsz, N_EXPERTS, cap, d), BF16),
        scratch_shapes=[pltpu.VMEM((rows, d), F32), pltpu.VMEM((rows, d), BF16),
                        pltpu.VMEM((nf, rows, tf), BF16), pltpu.SemaphoreType.DMA],
        compiler_params=_cparams(("arbitrary", "arbitrary", "arbitrary"), V7X_VMEM_LIMIT_BYTES),
        name="moe_ffn",
    )(gidx, gidx, h, w_gate, w_up, w_down)


def _combine_kernel(cap, tblk, starts_ref, x_ref, pos_ref, w_ref, y_hbm, o_ref, kbuf, sems):
    b = pl.program_id(0)
    tb = pl.program_id(1)
    ntb = pl.num_programs(1)
    step = b * ntb + tb
    cur = step % 2
    o_ref[0] = x_ref[0]

    def firsts(bb, tt):
        return [(starts_ref[bb, tt, e] // 16) * 16 for e in range(N_EXPERTS)]

    def offsets(first, p):
        lows = [first[e] + p * ROW_WIN for e in range(N_EXPERTS)]
        return lows, [pl.multiple_of(jnp.minimum(lo, cap - ROW_WIN), 16) for lo in lows]

    def window_copy(bb, e, off, buf):
        return pltpu.make_async_copy(y_hbm.at[bb, e, pl.ds(off, ROW_WIN), :],
                                     kbuf.at[buf, pl.ds(e * ROW_WIN, ROW_WIN), :], sems.at[buf])

    def start_pass(bb, first, p, buf):
        _, offs = offsets(first, p)
        for e in range(N_EXPERTS):
            window_copy(bb, e, offs[e], buf).start()

    def finish_pass(first, p, buf):
        lows, offs = offsets(first, p)
        lane = lax.broadcasted_iota(I32, (tblk, V7X_LANES), 1)
        slot = lane % ROW_WIN
        head = lane < ROW_WIN
        tiles = []
        for g in range(N_EXPERTS * ROW_WIN // V7X_LANES):
            ea, eb = 2 * g, 2 * g + 1
            pe = jnp.where(head, pos_ref[0, :, ea:ea + 1], pos_ref[0, :, eb:eb + 1])
            we = jnp.where(head, w_ref[0, :, ea:ea + 1], w_ref[0, :, eb:eb + 1])
            off = jnp.where(head, offs[ea], offs[eb])
            low = jnp.where(head, lows[ea], lows[eb])
            hit = (pe == off + slot) & (pe >= low)
            tiles.append(jnp.where(hit, we, 0.0).astype(BF16))
        pmat = jnp.concatenate(tiles, axis=1)
        for e in range(N_EXPERTS):
            window_copy(b, e, offs[e], buf).wait()
        o_ref[0] += jnp.dot(pmat, kbuf[buf], preferred_element_type=F32)

    first = firsts(b, tb)
    npass = jnp.int32(1)
    for e in range(N_EXPERTS):
        npass = jnp.maximum(npass, (starts_ref[b, tb + 1, e] - first[e] + ROW_WIN - 1) // ROW_WIN)

    @pl.when(step == 0)
    def _():
        start_pass(b, first, 0, cur)

    @pl.when(step + 1 < pl.num_programs(0) * ntb)
    def _():
        wrap = tb + 1 == ntb
        nb = jnp.where(wrap, b + 1, b)
        nt = jnp.where(wrap, 0, tb + 1)
        start_pass(nb, firsts(nb, nt), 0, 1 - cur)

    finish_pass(first, 0, cur)

    def extra_pass(p, carry):
        start_pass(b, first, p, cur)
        finish_pass(first, p, cur)
        return carry

    lax.fori_loop(1, npass, extra_pass, 0)


def _combine(x, pos, wsel, starts, y, tblk):
    bsz, t, d = x.shape
    cap = y.shape[2]
    return pl.pallas_call(
        functools.partial(_combine_kernel, cap, tblk),
        grid=(bsz, t // tblk),
        in_specs=[pl.BlockSpec(memory_space=pltpu.SMEM),
                  pl.BlockSpec((1, tblk, d), lambda b, i: (b, i, 0)),
                  pl.BlockSpec((1, tblk, V7X_LANES), lambda b, i: (b, i, 0)),
                  pl.BlockSpec((1, tblk, V7X_LANES), lambda b, i: (b, i, 0)),
                  pl.BlockSpec(memory_space=pl.ANY)],
        out_specs=pl.BlockSpec((1, tblk, d), lambda b, i: (b, i, 0)),
        out_shape=jax.ShapeDtypeStruct((bsz, t, d), F32),
        scratch_shapes=[pltpu.VMEM((2, N_EXPERTS * ROW_WIN, d), BF16), pltpu.SemaphoreType.DMA((2,))],
        compiler_params=_cparams(("arbitrary", "arbitrary")),
        name="moe_combine",
    )(starts, x, pos, wsel, y)


def _tiles(n, t):
    pick = lambda want, total: want if total % want == 0 else total
    return dict(
        tm_proj=pick(1024, n), tn_proj=768,
        tq=pick(2048, t), tk=pick(512, t), lb=pick(512, t),
        tm_gsum=pick(1024, n), tn_gsum=256, tm_oproj=pick(512, n),
        nf_ffn=4, nd_ffn=4,
        tblk=pick(256, t),
    )


def kernel(x, positions, norm_mix, w_in, conv_a, q_norm, k_norm, lambda_q1, lambda_k1, lambda_q2, lambda_k2, subln, conv_c, a_log_f, a_log_b, dt_bias_f, dt_bias_b, o_norm, w_out_a, w_out_b, w_out_c, w_gate, b_gate, w_o, norm_ffn, w_router, w_e_gate, w_e_up, w_e_down):
    bsz, t, d = x.shape
    n = bsz * t
    depth = w_in.shape[0]
    ts = _tiles(n, t)
    tables = _rope_tables(positions)
    pad_small = jnp.zeros((d, V7X_LANES - N_SMALL), F32)
    tile2 = lambda g: jnp.concatenate([g, g]).reshape(1, V7X_LANES)
    pad64 = lambda vv: jnp.concatenate([vv, jnp.zeros_like(vv)])
    w_in_t = jnp.swapaxes(w_in, 1, 2)

    xf = x.reshape(n, d)
    for l in range(depth):
        lam_init = 0.8 - 0.6 * math.exp(-0.3 * l)
        w_small = jnp.concatenate([w_in[l, :, N_MAIN:], pad_small], axis=1)
        main, small = _norm_matmul(xf, norm_mix[l], w_in_t, l, N_MAIN, w_small,
                                   ts["tm_proj"], ts["tn_proj"])
        main3 = main.reshape(bsz, t, N_MAIN)
        small3 = small.reshape(bsz, t, V7X_LANES)

        mix_a = _mixer_a(main3, conv_a[l])
        lamv = jnp.stack([pad64(lambda_q1[l]), pad64(lambda_k1[l]),
                          pad64(lambda_q2[l]), pad64(lambda_k2[l])])
        mix_b = _diff_attention(main3, tables, tile2(q_norm[l]), tile2(k_norm[l]), lamv,
                                subln[l].reshape(1, V7X_LANES), lam_init, ts["tq"], ts["tk"])
        qkv_c = _gdn_prep(main3, conv_c[l])
        o_f, o_b = _gdn(qkv_c, small3, a_log_f[l], dt_bias_f[l], a_log_b[l], dt_bias_b[l], ts["lb"])

        w_out = jnp.concatenate([w_out_a[l], w_out_b[l], w_out_c[l]], axis=0).astype(BF16)
        gsum = _gated_sum(xf, norm_mix[l], mix_a.reshape(n, W_A), mix_b.reshape(n, W_B),
                          o_f.reshape(n, W_C), o_b.reshape(n, W_C), main, o_norm[l],
                          w_out, w_gate[l].astype(BF16), b_gate[l].reshape(1, N_BRANCH * d),
                          ts["tm_gsum"], ts["tn_gsum"])
        x1 = _out_proj(xf, gsum, w_o[l].astype(BF16), ts["tm_oproj"])
        h, aff = _router(x1, norm_ffn[l], w_router[l], ts["tm_oproj"])
        idx, pos, wsel, starts = _select(aff, bsz, t, t // ts["tblk"])
        y = _expert_ffn(idx, h, w_e_gate, w_e_up, w_e_down, l, bsz, t, ts["nf_ffn"], ts["nd_ffn"])
        xf = _combine(x1.reshape(bsz, t, d), pos, wsel, starts, y, ts["tblk"]).reshape(n, d)
    return xf.reshape(bsz, t, d)
```
